```python
import jax, jax.numpy as jnp
from jax import lax
import numpy as np

D_MODEL = 1024
BATCH = 16
SEQ = 4096
DEPTH = 2

CHUNK = 64
N_MIXERS = 2
RET_HEADS = 4
RET_DK = D_MODEL // RET_HEADS
RET_DV = 2 * RET_DK
RET_QK = RET_HEADS * RET_DK
RET_VW = RET_HEADS * RET_DV
ROPE_BASE = 10000.0
ATT_HEADS = 16
ATT_DH = D_MODEL // ATT_HEADS
PAST_CHUNKS = 8
BAND_PAST = PAST_CHUNKS * CHUNK
BAND = (PAST_CHUNKS + 1) * CHUNK
MAX_REL = 256
REL_TABLE = MAX_REL + CHUNK
D_FF = 4 * D_MODEL
EPS = 1e-6

kernel_name = "hybrid_retention_chunkattn_encoder"


def _rmsnorm(x, g):
    xf = x.astype(jnp.float32)
    y = xf * lax.rsqrt(jnp.mean(xf * xf, axis=-1, keepdims=True) + EPS)
    return (y * g.astype(jnp.float32)).astype(x.dtype)


def _rope(t, pos):
    d = t.shape[-1]
    half = d // 2
    inv = jnp.exp(-jnp.log(ROPE_BASE) * jnp.arange(half, dtype=jnp.float32) / half)
    ang = pos[:, None] * inv[None, :]
    cos = jnp.cos(ang)[None, :, None, :]
    sin = jnp.sin(ang)[None, :, None, :]
    tf = t.astype(jnp.float32)
    t1, t2 = tf[..., :half], tf[..., half:]
    return jnp.concatenate([t1 * cos - t2 * sin, t1 * sin + t2 * cos], axis=-1).astype(t.dtype)


def _retention(h, w_in, gn_g, w_out):
    B, S, _ = h.shape
    nc = S // CHUNK
    proj = h @ w_in
    q, k, v, g = jnp.split(proj, [RET_QK, 2 * RET_QK, 2 * RET_QK + RET_VW], axis=-1)
    q = q.reshape(B, S, RET_HEADS, RET_DK)
    k = k.reshape(B, S, RET_HEADS, RET_DK)
    v = v.reshape(B, S, RET_HEADS, RET_DV)
    pos = jnp.arange(S, dtype=jnp.float32)
    q = _rope(q, pos)
    k = _rope(k, pos) * (RET_DK ** -0.5)

    def to_chunks(t):
        return t.reshape(B, nc, CHUNK, RET_HEADS, t.shape[-1]).transpose(1, 0, 3, 2, 4)

    qc, kc, vc = to_chunks(q), to_chunks(k), to_chunks(v)
    dt = q.dtype
    log_gamma = jnp.log1p(-jnp.exp2(-5.0 - jnp.arange(RET_HEADS, dtype=jnp.float32)))
    idx = jnp.arange(CHUNK, dtype=jnp.float32)
    intra_decay = jnp.exp(log_gamma[:, None, None] * jnp.abs(idx[:, None] - idx[None, :])).astype(dt)
    key_decay = jnp.exp(log_gamma[:, None] * (CHUNK - 1 - idx)[None, :]).astype(dt)
    query_decay = jnp.exp(log_gamma[:, None] * (idx + 1.0)[None, :]).astype(dt)
    chunk_decay = jnp.exp(log_gamma * CHUNK).astype(dt)

    def step(state, inp):
        qb, kb, vb = inp
        scores = jnp.einsum('bhid,bhjd->bhij', qb, kb) * intra_decay[None]
        o_intra = jnp.einsum('bhij,bhje->bhie', scores, vb)
        o_cross = jnp.einsum('bhid,bhde->bhie', qb, state) * query_decay[None, :, :, None]
        new_state = (state * chunk_decay[None, :, None, None]
                     + jnp.einsum('bhjd,bhje->bhde', kb * key_decay[None, :, :, None], vb))
        return new_state, o_intra + o_cross

    state0 = jnp.zeros((B, RET_HEADS, RET_DK, RET_DV), dt)
    _, o = lax.scan(step, state0, (qc, kc, vc))
    o = o.transpose(1, 0, 3, 2, 4).reshape(B, S, RET_HEADS, RET_DV).astype(jnp.float32)
    mu = jnp.mean(o, axis=-1, keepdims=True)
    var = jnp.mean(jnp.square(o - mu), axis=-1, keepdims=True)
    o = ((o - mu) * lax.rsqrt(var + EPS)).reshape(B, S, RET_VW) * gn_g.astype(jnp.float32)
    y = (jax.nn.silu(g.astype(jnp.float32)) * o).astype(h.dtype)
    return y @ w_out


def _chunk_attention(h, w_in, rel_bias, w_out):
    B, S, _ = h.shape
    nc = S // CHUNK
    q, k, v = jnp.split(h @ w_in, 3, axis=-1)
    q = q.reshape(B, S, ATT_HEADS, ATT_DH)
    k = k.reshape(B, S, ATT_HEADS, ATT_DH)
    v = v.reshape(B, S, ATT_HEADS, ATT_DH)
    pad = ((0, 0), (BAND_PAST, 0), (0, 0), (0, 0))
    kp = jnp.pad(k, pad)
    vp = jnp.pad(v, pad)
    qi = jnp.arange(CHUNK)[:, None]
    kj = jnp.arange(BAND)[None, :]
    rel = kj - BAND_PAST - qi
    bidx = jnp.maximum(rel, -MAX_REL) + MAX_REL
    bias = rel_bias[:, bidx].astype(jnp.float32)
    scale = ATT_DH ** -0.5

    def one_chunk(c):
        start = c * CHUNK
        qb = lax.dynamic_slice_in_dim(q, start, CHUNK, axis=1)
        kb = lax.dynamic_slice_in_dim(kp, start, BAND, axis=1)
        vb = lax.dynamic_slice_in_dim(vp, start, BAND, axis=1)
        s = jnp.einsum('bqhd,bkhd->bhqk', qb, kb).astype(jnp.float32) * scale + bias[None]
        valid = (start - BAND_PAST + jnp.arange(BAND)) >= 0
        s = jnp.where(valid[None, None, None, :], s, -jnp.inf)
        p = jax.nn.softmax(s, axis=-1).astype(vb.dtype)
        return jnp.einsum('bhqk,bkhd->bqhd', p, vb)

    o = lax.map(one_chunk, jnp.arange(nc))
    o = o.transpose(1, 0, 2, 3, 4).reshape(B, S, ATT_HEADS * ATT_DH)
    return o @ w_out


def _sqrelu_mlp(h, w1, w2):
    u = jax.nn.relu(h @ w1)
    return (u * u) @ w2


def setup_inputs(seed: int = 0) -> dict:
    key = jax.random.key(seed)
    ks = jax.random.split(key, 12)
    n_ret = (DEPTH + 1) // 2
    n_att = DEPTH // 2
    f = jnp.float32
    x = jax.random.normal(ks[0], (BATCH, SEQ, D_MODEL), f)
    mix_norm_g = 1.0 + 0.02 * jax.random.normal(ks[1], (DEPTH, D_MODEL), f)
    ret_w_in = jax.random.normal(ks[2], (n_ret, D_MODEL, 2 * RET_QK + 2 * RET_VW), f) * D_MODEL ** -0.5
    ret_gn_g = 1.0 + 0.02 * jax.random.normal(ks[3], (n_ret, RET_VW), f)
    ret_w_out = jax.random.normal(ks[4], (n_ret, RET_VW, D_MODEL), f) * RET_VW ** -0.5
    att_w_in = jax.random.normal(ks[5], (n_att, D_MODEL, 3 * D_MODEL), f) * D_MODEL ** -0.5
    att_rel_bias = 0.1 * jax.random.normal(ks[6], (n_att, ATT_HEADS, REL_TABLE), f)
    att_w_out = jax.random.normal(ks[7], (n_att, D_MODEL, D_MODEL), f) * D_MODEL ** -0.5
    mlp_norm_g = 1.0 + 0.02 * jax.random.normal(ks[8], (DEPTH, D_MODEL), f)
    mlp_w1 = jax.random.normal(ks[9], (DEPTH, D_MODEL, D_FF), f) * D_MODEL ** -0.5
    mlp_w2 = jax.random.normal(ks[10], (DEPTH, D_FF, D_MODEL), f) * D_FF ** -0.5
    final_norm_g = 1.0 + 0.02 * jax.random.normal(ks[11], (D_MODEL,), f)
    return {"x": x, "mix_norm_g": mix_norm_g, "ret_w_in": ret_w_in, "ret_gn_g": ret_gn_g,
            "ret_w_out": ret_w_out, "att_w_in": att_w_in, "att_rel_bias": att_rel_bias,
            "att_w_out": att_w_out, "mlp_norm_g": mlp_norm_g, "mlp_w1": mlp_w1,
            "mlp_w2": mlp_w2, "final_norm_g": final_norm_g}


def reference(x, mix_norm_g, ret_w_in, ret_gn_g, ret_w_out, att_w_in, att_rel_bias,
              att_w_out, mlp_norm_g, mlp_w1, mlp_w2, final_norm_g):
    h = x
    for i in range(DEPTH):
        hn = _rmsnorm(h, mix_norm_g[i])
        j = i // N_MIXERS
        if i % N_MIXERS == 0:
            h = h + _retention(hn, ret_w_in[j], ret_gn_g[j], ret_w_out[j])
        else:
            h = h + _chunk_attention(hn, att_w_in[j], att_rel_bias[j], att_w_out[j])
        h = h + _sqrelu_mlp(_rmsnorm(h, mlp_norm_g[i]), mlp_w1[i], mlp_w2[i])
    return _rmsnorm(h, final_norm_g)
```

```python
import functools

import numpy as np
import jax
import jax.numpy as jnp
from jax import lax
from jax.experimental import pallas as pl
from jax.experimental.pallas import tpu as pltpu

CHUNK = 64
RET_HEADS = 4
ATT_HEADS = 16
PAST_CHUNKS = 8
MAX_REL = 256
ROPE_BASE = 10000.0
EPS = 1e-6

V7X_LANES = 128
V7X_VMEM_LIMIT_BYTES = 60000 * 1024

ROW_BLOCK = 512
SEQ_BLOCK = 256
FF_BLOCK = 1024

BF16 = jnp.bfloat16
F32 = jnp.float32


def _params(*semantics):
    return pltpu.CompilerParams(dimension_semantics=semantics,
                                vmem_limit_bytes=V7X_VMEM_LIMIT_BYTES)


def _resident(shape):
    zeros = (0,) * len(shape)
    return pl.BlockSpec(shape, lambda *_: zeros, pipeline_mode=pl.Buffered(1))


def _rmsnorm_bf16(x, g):
    ms = jnp.mean(x * x, axis=-1, keepdims=True)
    return (x * lax.rsqrt(ms + EPS) * g).astype(BF16)


def _dot(a, b):
    return jnp.dot(a, b, preferred_element_type=F32)


def _dot_nt(a, b):
    return lax.dot_general(a, b, (((1,), (1,)), ((), ())), preferred_element_type=F32)


def _dot_tn(a, b):
    return lax.dot_general(a, b, (((0,), (0,)), ((), ())), preferred_element_type=F32)


def _ret_proj_kernel(x_ref, g_ref, w_ref, cos_ref, sin_ref, q_ref, k_ref, v_ref, gate_ref,
                     *, heads, dk, k_scale):
    hn = _rmsnorm_bf16(x_ref[...], g_ref[...])
    cos = cos_ref[...]
    sin = sin_ref[...]
    half = dk // 2
    qk_width = heads * dk

    def rope_store(out_ref, col0, scale):
        for h in range(heads):
            t = _dot(hn, w_ref[:, col0 + h * dk: col0 + (h + 1) * dk])
            t1 = t[:, :half]
            t2 = t[:, half:]
            out_ref[:, h * dk: h * dk + half] = ((t1 * cos - t2 * sin) * scale).astype(BF16)
            out_ref[:, h * dk + half: (h + 1) * dk] = ((t1 * sin + t2 * cos) * scale).astype(BF16)

    rope_store(q_ref, 0, 1.0)
    rope_store(k_ref, qk_width, k_scale)
    vw = v_ref.shape[1]
    step = 512
    for c in range(0, vw, step):
        v_ref[:, c:c + step] = _dot(hn, w_ref[:, 2 * qk_width + c: 2 * qk_width + c + step]).astype(BF16)
    for c in range(0, vw, step):
        gate_ref[:, c:c + step] = _dot(
            hn, w_ref[:, 2 * qk_width + vw + c: 2 * qk_width + vw + c + step]).astype(BF16)


def _ret_proj(x2, g, w_bf16, cos, sin, *, seq):
    tokens, d = x2.shape
    n_all = w_bf16.shape[1]
    qk = d
    vw = (n_all - 2 * qk) // 2
    dk = qk // RET_HEADS
    tm = ROW_BLOCK
    blocks_per_seq = seq // tm
    row = lambda i: (i, 0)
    pos = lambda i: (i % blocks_per_seq, 0)
    return pl.pallas_call(
        functools.partial(_ret_proj_kernel, heads=RET_HEADS, dk=dk, k_scale=dk ** -0.5),
        grid=(tokens // tm,),
        in_specs=[
            pl.BlockSpec((tm, d), row),
            _resident((1, d)),
            _resident((d, n_all)),
            pl.BlockSpec((tm, dk // 2), pos),
            pl.BlockSpec((tm, dk // 2), pos),
        ],
        out_specs=[
            pl.BlockSpec((tm, qk), row),
            pl.BlockSpec((tm, qk), row),
            pl.BlockSpec((tm, vw), row),
            pl.BlockSpec((tm, vw), row),
        ],
        out_shape=[
            jax.ShapeDtypeStruct((tokens, qk), BF16),
            jax.ShapeDtypeStruct((tokens, qk), BF16),
            jax.ShapeDtypeStruct((tokens, vw), BF16),
            jax.ShapeDtypeStruct((tokens, vw), BF16),
        ],
        compiler_params=_params("parallel"),
        name="ret_proj",
    )(x2, g, w_bf16, cos, sin)


def _ret_core_kernel(q_ref, k_ref, v_ref, gate_ref, x_ref, mask_ref, qdec_ref, kdec_ref,
                     gn_ref, wout_ref, o_ref, state_ref, *, heads, dk, dv, block_decay):
    @pl.when(pl.program_id(1) == 0)
    def _():
        state_ref[...] = jnp.zeros_like(state_ref)

    acc = x_ref[...]
    for h in range(heads):
        qh = q_ref[:, h * dk:(h + 1) * dk]
        kh = k_ref[:, h * dk:(h + 1) * dk]
        vh = v_ref[:, h * dv:(h + 1) * dv]
        state = state_ref[h]
        scores = _dot_nt(qh, kh) * mask_ref[h]
        o = _dot(scores.astype(BF16), vh)
        o = o + _dot(qh, state.astype(BF16)) * qdec_ref[h]
        k_dec = (kh.astype(F32) * kdec_ref[h]).astype(BF16)
        state_ref[h] = state * block_decay[h] + _dot_tn(k_dec, vh)

        mu = jnp.mean(o, axis=-1, keepdims=True)
        cen = o - mu
        var = jnp.mean(cen * cen, axis=-1, keepdims=True)
        gate = gate_ref[:, h * dv:(h + 1) * dv].astype(F32)
        silu = gate * (1.0 / (1.0 + jnp.exp(-gate)))
        y = cen * lax.rsqrt(var + EPS) * gn_ref[:, h * dv:(h + 1) * dv] * silu
        acc = acc + _dot(y.astype(BF16), wout_ref[h * dv:(h + 1) * dv, :])
    o_ref[...] = acc


def _ret_core(q, k, v, gate, x2, tables, gn_g, wout_bf16, *, batch, seq):
    tokens, d = x2.shape
    qk = q.shape[1]
    vw = v.shape[1]
    dk = qk // RET_HEADS
    dv = vw // RET_HEADS
    blk = SEQ_BLOCK
    nblk = seq // blk
    mask, qdec, kdec, block_decay = tables
    row = lambda b, j: (b * nblk + j, 0)
    return pl.pallas_call(
        functools.partial(_ret_core_kernel, heads=RET_HEADS, dk=dk, dv=dv, block_decay=block_decay),
        grid=(batch, nblk),
        in_specs=[
            pl.BlockSpec((blk, qk), row),
            pl.BlockSpec((blk, qk), row),
            pl.BlockSpec((blk, vw), row),
            pl.BlockSpec((blk, vw), row),
            pl.BlockSpec((blk, d), row),
            _resident(mask.shape),
            _resident(qdec.shape),
            _resident(kdec.shape),
            _resident((1, vw)),
            _resident((vw, d)),
        ],
        out_specs=pl.BlockSpec((blk, d), row),
        out_shape=jax.ShapeDtypeStruct((tokens, d), F32),
        scratch_shapes=[pltpu.VMEM((RET_HEADS, dk, dv), F32)],
        compiler_params=_params("parallel", "arbitrary"),
        name="ret_core",
    )(q, k, v, gate, x2, mask, qdec, kdec, gn_g, wout_bf16)


def _retention_tables(dk, dv):
    blk = SEQ_BLOCK
    log_gamma = np.log1p(-np.exp2(-5.0 - np.arange(RET_HEADS, dtype=np.float64)))
    n = np.arange(blk)
    diff = n[:, None] - n[None, :]
    same = (n[:, None] // CHUNK) == (n[None, :] // CHUNK)
    earlier = (n[None, :] // CHUNK) < (n[:, None] // CHUNK)
    lg = log_gamma[:, None, None]
    mask = np.where(same[None], np.exp(lg * np.abs(diff)[None]),
                    np.where(earlier[None], np.exp(lg * diff[None]), 0.0))
    qdec = np.exp(log_gamma[:, None] * (n + 1.0)[None, :])
    kdec = np.exp(log_gamma[:, None] * (blk - 1.0 - n)[None, :])
    block_decay = tuple(float(c) for c in np.exp(log_gamma * blk))
    qdec = np.broadcast_to(qdec[:, :, None], (RET_HEADS, blk, dv))
    kdec = np.broadcast_to(kdec[:, :, None], (RET_HEADS, blk, dk))
    as_f32 = lambda a: jnp.asarray(np.ascontiguousarray(a), dtype=F32)
    return as_f32(mask), as_f32(qdec), as_f32(kdec), block_decay


def _rope_tables(seq, dk):
    half = dk // 2
    inv = np.exp(-np.log(ROPE_BASE) * np.arange(half, dtype=np.float64) / half)
    ang = np.arange(seq, dtype=np.float64)[:, None] * inv[None, :]
    return jnp.asarray(np.cos(ang), dtype=F32), jnp.asarray(np.sin(ang), dtype=F32)


def _mlp_kernel(x_ref, g_ref, w1_ref, w2_ref, *rest, final_norm):
    if final_norm:
        gf_ref, o_ref = rest
    else:
        (o_ref,) = rest
    x = x_ref[...]
    hn = _rmsnorm_bf16(x, g_ref[...])
    acc = x
    d_ff = w1_ref.shape[1]
    for c in range(0, d_ff, FF_BLOCK):
        u = jnp.maximum(_dot(hn, w1_ref[:, c:c + FF_BLOCK]), 0.0)
        acc = acc + _dot((u * u).astype(BF16), w2_ref[c:c + FF_BLOCK, :])
    if final_norm:
        ms = jnp.mean(acc * acc, axis=-1, keepdims=True)
        acc = acc * lax.rsqrt(ms + EPS) * gf_ref[...]
    o_ref[...] = acc


def _mlp(x2, g, w1_bf16, w2_bf16, final_g=None):
    tokens, d = x2.shape
    d_ff = w1_bf16.shape[1]
    tm = ROW_BLOCK
    row = lambda i: (i, 0)
    in_specs = [pl.BlockSpec((tm, d), row), _resident((1, d)), _resident((d, d_ff)), _resident((d_ff, d))]
    args = [x2, g, w1_bf16, w2_bf16]
    if final_g is not None:
        in_specs.append(_resident((1, d)))
        args.append(final_g)
    return pl.pallas_call(
        functools.partial(_mlp_kernel, final_norm=final_g is not None),
        grid=(tokens // tm,),
        in_specs=in_specs,
        out_specs=pl.BlockSpec((tm, d), row),
        out_shape=jax.ShapeDtypeStruct((tokens, d), F32),
        compiler_params=_params("parallel"),
        name="mlp_final" if final_g is not None else "mlp",
    )(*args)


def _att_proj_kernel(x_ref, g_ref, wqt_ref, wk_ref, wvt_ref, qt_ref, k_ref, vt_ref, *, q_scale):
    hn = _rmsnorm_bf16(x_ref[...], g_ref[...])
    d = x_ref.shape[1]
    step = 512
    for c in range(0, d, step):
        qt_ref[c:c + step, :] = (_dot_nt(wqt_ref[c:c + step, :], hn) * q_scale).astype(BF16)
    for c in range(0, d, step):
        k_ref[:, c:c + step] = _dot(hn, wk_ref[:, c:c + step]).astype(BF16)
    for c in range(0, d, step):
        vt_ref[c:c + step, :] = _dot_nt(wvt_ref[c:c + step, :], hn).astype(BF16)


def _att_proj(x2, g, wqt, wk, wvt):
    tokens, d = x2.shape
    tm = ROW_BLOCK
    row = lambda i: (i, 0)
    col = lambda i: (0, i)
    return pl.pallas_call(
        functools.partial(_att_proj_kernel, q_scale=(d // ATT_HEADS) ** -0.5),
        grid=(tokens // tm,),
        in_specs=[pl.BlockSpec((tm, d), row), _resident((1, d)),
                  _resident((d, d)), _resident((d, d)), _resident((d, d))],
        out_specs=[pl.BlockSpec((d, tm), col), pl.BlockSpec((tm, d), row), pl.BlockSpec((d, tm), col)],
        out_shape=[jax.ShapeDtypeStruct((d, tokens), BF16),
                   jax.ShapeDtypeStruct((tokens, d), BF16),
                   jax.ShapeDtypeStruct((d, tokens), BF16)],
        compiler_params=_params("parallel"),
        name="att_proj",
    )(x2, g, wqt, wk, wvt)


def _att_core_kernel(qt_ref, k0_ref, k1_ref, k2_ref, v0_ref, v1_ref, v2_ref, x_ref, bias_ref,
                     wout_ref, o_ref, ot_ref, *, heads, dh):
    j = pl.program_id(1)
    neg_inf = jnp.float32(-jnp.inf)
    pens = (jnp.where(j >= 2, 0.0, neg_inf), jnp.where(j >= 1, 0.0, neg_inf), None)
    k_refs = (k0_ref, k1_ref, k2_ref)
    v_refs = (v0_ref, v1_ref, v2_ref)
    blk = qt_ref.shape[1]
    zeros = jnp.zeros((dh, blk), BF16)
    for hp in range(heads // 2):
        lo = hp * 2 * dh
        q_even = jnp.concatenate([qt_ref[lo:lo + dh, :], zeros], axis=0)
        q_odd = jnp.concatenate([zeros, qt_ref[lo + dh:lo + 2 * dh, :]], axis=0)
        for par, q_m in ((0, q_even), (1, q_odd)):
            h = 2 * hp + par
            s = []
            for t in range(3):
                st = _dot(k_refs[t][:, lo:lo + 2 * dh], q_m) + bias_ref[h, t * blk:(t + 1) * blk, :]
                if pens[t] is not None:
                    st = st + pens[t]
                s.append(st)
            m = jnp.maximum(jnp.maximum(jnp.max(s[0], axis=0, keepdims=True),
                                        jnp.max(s[1], axis=0, keepdims=True)),
                            jnp.max(s[2], axis=0, keepdims=True))
            l = jnp.zeros_like(m)
            o_t = jnp.zeros((dh, blk), F32)
            for t in range(3):
                p = jnp.exp(s[t] - m)
                l = l + jnp.sum(p, axis=0, keepdims=True)
                o_t = o_t + _dot(v_refs[t][h * dh:(h + 1) * dh, :], p.astype(BF16))
            ot_ref[h * dh:(h + 1) * dh, :] = (o_t * (1.0 / l)).astype(BF16)
    o_ref[...] = x_ref[...] + _dot_tn(ot_ref[...], wout_ref[...])


def _att_core(qt, k, vt, x2, bias_t, wout_bf16, *, batch, seq):
    tokens, d = x2.shape
    blk = SEQ_BLOCK
    nblk = seq // blk
    dh = d // ATT_HEADS

    def k_map(back):
        return lambda b, j: (b * nblk + jnp.maximum(j - back, 0), 0)

    def v_map(back):
        return lambda b, j: (0, b * nblk + jnp.maximum(j - back, 0))

    row = lambda b, j: (b * nblk + j, 0)
    return pl.pallas_call(
        functools.partial(_att_core_kernel, heads=ATT_HEADS, dh=dh),
        grid=(batch, nblk),
        in_specs=[
            pl.BlockSpec((d, blk), lambda b, j: (0, b * nblk + j)),
            pl.BlockSpec((blk, d), k_map(2)),
            pl.BlockSpec((blk, d), k_map(1)),
            pl.BlockSpec((blk, d), k_map(0)),
            pl.BlockSpec((d, blk), v_map(2)),
            pl.BlockSpec((d, blk), v_map(1)),
            pl.BlockSpec((d, blk), v_map(0)),
            pl.BlockSpec((blk, d), row),
            _resident(bias_t.shape),
            _resident((d, d)),
        ],
        out_specs=pl.BlockSpec((blk, d), row),
        out_shape=jax.ShapeDtypeStruct((tokens, d), F32),
        scratch_shapes=[pltpu.VMEM((d, blk), BF16)],
        compiler_params=_params("parallel", "arbitrary"),
        name="att_core",
    )(qt, k, k, k, vt, vt, vt, x2, bias_t, wout_bf16)


def _attention_bias_table(rel_bias):
    blk = SEQ_BLOCK
    past = PAST_CHUNKS * CHUNK
    assert past == 2 * blk
    m = np.arange(3 * blk)[:, None]
    n = np.arange(blk)[None, :]
    rel = m - past - n
    key_chunk = m // CHUNK - PAST_CHUNKS
    query_chunk = n // CHUNK
    valid = (key_chunk <= query_chunk) & (key_chunk >= query_chunk - PAST_CHUNKS)
    idx = np.clip(np.maximum(rel, -MAX_REL) + MAX_REL, 0, rel_bias.shape[1] - 1)
    table = rel_bias.astype(F32)[:, idx]
    return jnp.where(jnp.asarray(valid)[None], table, -jnp.inf)


def kernel(x, mix_norm_g, ret_w_in, ret_gn_g, ret_w_out, att_w_in, att_rel_bias, att_w_out,
           mlp_norm_g, mlp_w1, mlp_w2, final_norm_g):
    batch, seq, d = x.shape
    depth = mix_norm_g.shape[0]
    assert seq % ROW_BLOCK == 0 and seq % SEQ_BLOCK == 0 and SEQ_BLOCK % CHUNK == 0
    h = x.reshape(batch * seq, d)
    row_vec = lambda v: v.reshape(1, -1).astype(F32)

    for i in range(depth):
        jdx = i // 2
        if i % 2 == 0:
            w_in = ret_w_in[jdx].astype(BF16)
            qk = d
            vw = (w_in.shape[1] - 2 * qk) // 2
            dk, dv = qk // RET_HEADS, vw // RET_HEADS
            cos, sin = _rope_tables(seq, dk)
            q, k, v, gate = _ret_proj(h, row_vec(mix_norm_g[i]), w_in, cos, sin, seq=seq)
            h = _ret_core(q, k, v, gate, h, _retention_tables(dk, dv), row_vec(ret_gn_g[jdx]),
                          ret_w_out[jdx].astype(BF16), batch=batch, seq=seq)
        else:
            w_in = att_w_in[jdx]
            wqt = w_in[:, :d].T.astype(BF16)
            wk = w_in[:, d:2 * d].astype(BF16)
            wvt = w_in[:, 2 * d:].T.astype(BF16)
            qt, k, vt = _att_proj(h, row_vec(mix_norm_g[i]), wqt, wk, wvt)
            h = _att_core(qt, k, vt, h, _attention_bias_table(att_rel_bias[jdx]),
                          att_w_out[jdx].astype(BF16), batch=batch, seq=seq)
        last = i == depth - 1
        h = _mlp(h, row_vec(mlp_norm_g[i]), mlp_w1[i].astype(BF16), mlp_w2[i].astype(BF16),
                 row_vec(final_norm_g) if last else None)
    if depth == 0:
        raise NotImplementedError("depth 0 has no Pallas stage")
    return h.reshape(batch, seq, d)
```

```python
import functools

import numpy as np
import jax
import jax.numpy as jnp
from jax import lax
from jax.experimental import pallas as pl
from jax.experimental.pallas import tpu as pltpu

CHUNK = 64
RET_HEADS = 4
ATT_HEADS = 16
PAST_CHUNKS = 8
MAX_REL = 256
ROPE_BASE = 10000.0
EPS = 1e-6

V7X_LANES = 128
V7X_VMEM_LIMIT_BYTES = 60000 * 1024

ROW_BLOCK = 512
SEQ_BLOCK = 256
FF_BLOCK = 1024

BF16 = jnp.bfloat16
F32 = jnp.float32


def _params(*semantics):
    return pltpu.CompilerParams(dimension_semantics=semantics,
                                vmem_limit_bytes=V7X_VMEM_LIMIT_BYTES)


def _resident(shape):
    zeros = (0,) * len(shape)
    return pl.BlockSpec(shape, lambda *_: zeros, pipeline_mode=pl.Buffered(1))


def _rmsnorm_bf16(x, g):
    ms = jnp.mean(x * x, axis=-1, keepdims=True)
    return (x * lax.rsqrt(ms + EPS) * g).astype(BF16)


def _dot(a, b):
    return jnp.dot(a, b, preferred_element_type=F32)


def _dot_nt(a, b):
    return lax.dot_general(a, b, (((1,), (1,)), ((), ())), preferred_element_type=F32)


def _dot_tn(a, b):
    return lax.dot_general(a, b, (((0,), (0,)), ((), ())), preferred_element_type=F32)


def _ret_proj_kernel(x_ref, g_ref, w_ref, cos_ref, sin_ref, q_ref, k_ref, v_ref, gate_ref,
                     *, heads, dk, k_scale):
    hn = _rmsnorm_bf16(x_ref[...], g_ref[...])
    cos = cos_ref[...]
    sin = sin_ref[...]
    half = dk // 2
    qk_width = heads * dk

    def rope_store(out_ref, col0, scale):
        for h in range(heads):
            t = _dot(hn, w_ref[:, col0 + h * dk: col0 + (h + 1) * dk])
            t1 = t[:, :half]
            t2 = t[:, half:]
            out_ref[:, h * dk: h * dk + half] = ((t1 * cos - t2 * sin) * scale).astype(BF16)
            out_ref[:, h * dk + half: (h + 1) * dk] = ((t1 * sin + t2 * cos) * scale).astype(BF16)

    rope_store(q_ref, 0, 1.0)
    rope_store(k_ref, qk_width, k_scale)
    vw = v_ref.shape[1]
    step = 512
    for c in range(0, vw, step):
        v_ref[:, c:c + step] = _dot(hn, w_ref[:, 2 * qk_width + c: 2 * qk_width + c + step]).astype(BF16)
    for c in range(0, vw, step):
        gate_ref[:, c:c + step] = _dot(
            hn, w_ref[:, 2 * qk_width + vw + c: 2 * qk_width + vw + c + step]).astype(BF16)


def _ret_proj(x2, g, w_bf16, cos, sin, *, seq):
    tokens, d = x2.shape
    n_all = w_bf16.shape[1]
    qk = d
    vw = (n_all - 2 * qk) // 2
    dk = qk // RET_HEADS
    tm = ROW_BLOCK
    blocks_per_seq = seq // tm
    row = lambda i: (i, 0)
    pos = lambda i: (i % blocks_per_seq, 0)
    return pl.pallas_call(
        functools.partial(_ret_proj_kernel, heads=RET_HEADS, dk=dk, k_scale=dk ** -0.5),
        grid=(tokens // tm,),
        in_specs=[
            pl.BlockSpec((tm, d), row),
            _resident((1, d)),
            _resident((d, n_all)),
            pl.BlockSpec((tm, dk // 2), pos),
            pl.BlockSpec((tm, dk // 2), pos),
        ],
        out_specs=[
            pl.BlockSpec((tm, qk), row),
            pl.BlockSpec((tm, qk), row),
            pl.BlockSpec((tm, vw), row),
            pl.BlockSpec((tm, vw), row),
        ],
        out_shape=[
            jax.ShapeDtypeStruct((tokens, qk), BF16),
            jax.ShapeDtypeStruct((tokens, qk), BF16),
            jax.ShapeDtypeStruct((tokens, vw), BF16),
            jax.ShapeDtypeStruct((tokens, vw), BF16),
        ],
        compiler_params=_params("parallel"),
        name="ret_proj",
    )(x2, g, w_bf16, cos, sin)


def _ret_core_kernel(q_ref, k_ref, v_ref, gate_ref, x_ref, mask_ref, qdec_ref, kdec_ref,
                     gn_ref, wout_ref, o_ref, state_ref, *, heads, dk, dv, block_decay):
    @pl.when(pl.program_id(1) == 0)
    def _():
        state_ref[...] = jnp.zeros_like(state_ref)

    acc = x_ref[...]
    for h in range(heads):
        qh = q_ref[:, h * dk:(h + 1) * dk]
        kh = k_ref[:, h * dk:(h + 1) * dk]
        vh = v_ref[:, h * dv:(h + 1) * dv]
        state = state_ref[h]
        scores = _dot_nt(qh, kh) * mask_ref[h]
        o = _dot(scores.astype(BF16), vh)
        o = o + _dot(qh, state.astype(BF16)) * qdec_ref[h]
        k_dec = (kh.astype(F32) * kdec_ref[h]).astype(BF16)
        state_ref[h] = state * block_decay[h] + _dot_tn(k_dec, vh)

        mu = jnp.mean(o, axis=-1, keepdims=True)
        cen = o - mu
        var = jnp.mean(cen * cen, axis=-1, keepdims=True)
        gate = gate_ref[:, h * dv:(h + 1) * dv].astype(F32)
        silu = gate * (1.0 / (1.0 + jnp.exp(-gate)))
        y = cen * lax.rsqrt(var + EPS) * gn_ref[:, h * dv:(h + 1) * dv] * silu
        acc = acc + _dot(y.astype(BF16), wout_ref[h * dv:(h + 1) * dv, :])
    o_ref[...] = acc


def _ret_core(q, k, v, gate, x2, tables, gn_g, wout_bf16, *, batch, seq):
    tokens, d = x2.shape
    qk = q.shape[1]
    vw = v.shape[1]
    dk = qk // RET_HEADS
    dv = vw // RET_HEADS
    blk = SEQ_BLOCK
    nblk = seq // blk
    mask, qdec, kdec, block_decay = tables
    row = lambda b, j: (b * nblk + j, 0)
    return pl.pallas_call(
        functools.partial(_ret_core_kernel, heads=RET_HEADS, dk=dk, dv=dv, block_decay=block_decay),
        grid=(batch, nblk),
        in_specs=[
            pl.BlockSpec((blk, qk), row),
            pl.BlockSpec((blk, qk), row),
            pl.BlockSpec((blk, vw), row),
            pl.BlockSpec((blk, vw), row),
            pl.BlockSpec((blk, d), row),
            _resident(mask.shape),
            _resident(qdec.shape),
            _resident(kdec.shape),
            _resident((1, vw)),
            _resident((vw, d)),
        ],
        out_specs=pl.BlockSpec((blk, d), row),
        out_shape=jax.ShapeDtypeStruct((tokens, d), F32),
        scratch_shapes=[pltpu.VMEM((RET_HEADS, dk, dv), F32)],
        compiler_params=_params("parallel", "arbitrary"),
        name="ret_core",
    )(q, k, v, gate, x2, mask, qdec, kdec, gn_g, wout_bf16)


def _retention_tables(dk, dv):
    blk = SEQ_BLOCK
    log_gamma = np.log1p(-np.exp2(-5.0 - np.arange(RET_HEADS, dtype=np.float64)))
    n = np.arange(blk)
    diff = n[:, None] - n[None, :]
    same = (n[:, None] // CHUNK) == (n[None, :] // CHUNK)
    earlier = (n[None, :] // CHUNK) < (n[:, None] // CHUNK)
    lg = log_gamma[:, None, None]
    mask = np.where(same[None], np.exp(lg * np.abs(diff)[None]),
                    np.where(earlier[None], np.exp(lg * diff[None]), 0.0))
    qdec = np.exp(log_gamma[:, None] * (n + 1.0)[None, :])
    kdec = np.exp(log_gamma[:, None] * (blk - 1.0 - n)[None, :])
    block_decay = tuple(float(c) for c in np.exp(log_gamma * blk))
    qdec = np.broadcast_to(qdec[:, :, None], (RET_HEADS, blk, dv))
    kdec = np.broadcast_to(kdec[:, :, None], (RET_HEADS, blk, dk))
    as_f32 = lambda a: jnp.asarray(np.ascontiguousarray(a), dtype=F32)
    return as_f32(mask), as_f32(qdec), as_f32(kdec), block_decay


def _rope_tables(seq, dk):
    half = dk // 2
    inv = np.exp(-np.log(ROPE_BASE) * np.arange(half, dtype=np.float64) / half)
    ang = np.arange(seq, dtype=np.float64)[:, None] * inv[None, :]
    return jnp.asarray(np.cos(ang), dtype=F32), jnp.asarray(np.sin(ang), dtype=F32)


def _mlp_kernel(x_ref, g_ref, w1_ref, w2_ref, *rest, final_norm):
    if final_norm:
        gf_ref, o_ref = rest
    else:
        (o_ref,) = rest
    x = x_ref[...]
    hn = _rmsnorm_bf16(x, g_ref[...])
    acc = x
    d_ff = w1_ref.shape[1]
    for c in range(0, d_ff, FF_BLOCK):
        u = jnp.maximum(_dot(hn, w1_ref[:, c:c + FF_BLOCK]), 0.0)
        acc = acc + _dot((u * u).astype(BF16), w2_ref[c:c + FF_BLOCK, :])
    if final_norm:
        ms = jnp.mean(acc * acc, axis=-1, keepdims=True)
        acc = acc * lax.rsqrt(ms + EPS) * gf_ref[...]
    o_ref[...] = acc


def _mlp(x2, g, w1_bf16, w2_bf16, final_g=None):
    tokens, d = x2.shape
    d_ff = w1_bf16.shape[1]
    tm = ROW_BLOCK
    row = lambda i: (i, 0)
    in_specs = [pl.BlockSpec((tm, d), row), _resident((1, d)), _resident((d, d_ff)), _resident((d_ff, d))]
    args = [x2, g, w1_bf16, w2_bf16]
    if final_g is not None:
        in_specs.append(_resident((1, d)))
        args.append(final_g)
    return pl.pallas_call(
        functools.partial(_mlp_kernel, final_norm=final_g is not None),
        grid=(tokens // tm,),
        in_specs=in_specs,
        out_specs=pl.BlockSpec((tm, d), row),
        out_shape=jax.ShapeDtypeStruct((tokens, d), F32),
        compiler_params=_params("parallel"),
        name="mlp_final" if final_g is not None else "mlp",
    )(*args)


def _att_proj_kernel(x_ref, g_ref, wqt_ref, wk_ref, wvt_ref, qt_ref, k_ref, vt_ref, *, q_scale):
    hn = _rmsnorm_bf16(x_ref[...], g_ref[...])
    d = x_ref.shape[1]
    step = 512
    for c in range(0, d, step):
        qt_ref[c:c + step, :] = (_dot_nt(wqt_ref[c:c + step, :], hn) * q_scale).astype(BF16)
    for c in range(0, d, step):
        k_ref[:, c:c + step] = _dot(hn, wk_ref[:, c:c + step]).astype(BF16)
    for c in range(0, d, step):
        vt_ref[c:c + step, :] = _dot_nt(wvt_ref[c:c + step, :], hn).astype(BF16)


def _att_proj(x2, g, wqt, wk, wvt):
    tokens, d = x2.shape
    tm = ROW_BLOCK
    row = lambda i: (i, 0)
    col = lambda i: (0, i)
    return pl.pallas_call(
        functools.partial(_att_proj_kernel, q_scale=(d // ATT_HEADS) ** -0.5),
        grid=(tokens // tm,),
        in_specs=[pl.BlockSpec((tm, d), row), _resident((1, d)),
                  _resident((d, d)), _resident((d, d)), _resident((d, d))],
        out_specs=[pl.BlockSpec((d, tm), col), pl.BlockSpec((tm, d), row), pl.BlockSpec((d, tm), col)],
        out_shape=[jax.ShapeDtypeStruct((d, tokens), BF16),
                   jax.ShapeDtypeStruct((tokens, d), BF16),
                   jax.ShapeDtypeStruct((d, tokens), BF16)],
        compiler_params=_params("parallel"),
        name="att_proj",
    )(x2, g, wqt, wk, wvt)


def _band_live_pieces(blk, sub):
    m = np.arange(3 * blk)[:, None]
    n = np.arange(blk)[None, :]
    key_chunk = m // CHUNK - PAST_CHUNKS
    query_chunk = n // CHUNK
    valid = (key_chunk <= query_chunk) & (key_chunk >= query_chunk - PAST_CHUNKS)
    pieces = valid.reshape(3, blk // sub, sub, blk // sub, sub).any(axis=(2, 4))
    return pieces.tolist()


def _att_core_kernel(qt_ref, k0_ref, k1_ref, k2_ref, v0_ref, v1_ref, v2_ref, x_ref, rel_ref,
                     wout_ref, o_ref, ot_ref, bias_ref, *, heads, dh):
    b = pl.program_id(0)
    j = pl.program_id(1)
    blk = qt_ref.shape[1]
    n_tiles = 3

    @pl.when((b == 0) & (j == 0))
    def _build_bias():
        r = lax.broadcasted_iota(jnp.int32, (blk, blk), 0)
        c = lax.broadcasted_iota(jnp.int32, (blk, blk), 1)
        shift = CHUNK.bit_length() - 1
        query_chunk = lax.shift_right_logical(c, shift)
        chunks_per_blk = blk // CHUNK

        def per_head(h, carry):
            for t in range(n_tiles):
                key_chunk = lax.shift_right_logical(r, shift) + (t * chunks_per_blk - PAST_CHUNKS)
                valid = (key_chunk <= query_chunk) & (key_chunk >= query_chunk - PAST_CHUNKS)
                row = jnp.broadcast_to(rel_ref[h, t:t + 1, :], (blk, 2 * blk))
                tile = pltpu.roll(row, blk, 1, stride=1, stride_axis=0)[:, :blk]
                bias_ref[h, t * blk:(t + 1) * blk, :] = jnp.where(valid, tile, -jnp.inf)
            return carry

        lax.fori_loop(0, heads, per_head, 0)

    k_refs = (k0_ref, k1_ref, k2_ref)
    v_refs = (v0_ref, v1_ref, v2_ref)
    exists = (j >= 2, j >= 1, None)
    zeros = jnp.zeros((dh, blk), BF16)
    ones = jnp.ones((16, blk), BF16)

    def pair_scores(hp):
        lo = hp * 2 * dh
        q_even = jnp.concatenate([qt_ref[lo:lo + dh, :], zeros], axis=0)
        q_odd = jnp.concatenate([zeros, qt_ref[lo + dh:lo + 2 * dh, :]], axis=0)
        q_pair = jnp.concatenate([q_even, q_odd], axis=1)
        return [_dot(k_refs[t][:, lo:lo + 2 * dh], q_pair) for t in range(n_tiles)]

    sub = V7X_LANES
    n_sub = blk // sub
    live = _band_live_pieces(blk, sub)

    def head_output(h, s_pair):
        par = h % 2
        p_pieces = [[[None] * n_sub for _ in range(n_sub)] for _ in range(n_tiles)]
        for ci in range(n_sub):
            cols = slice(ci * sub, (ci + 1) * sub)
            s = {}
            m = None
            for t in range(n_tiles):
                for ri in range(n_sub):
                    if not live[t][ri][ci]:
                        continue
                    rows = slice(ri * sub, (ri + 1) * sub)
                    piece = (s_pair[t][rows, par * blk + ci * sub:par * blk + (ci + 1) * sub]
                             + bias_ref[h, t * blk + ri * sub:t * blk + (ri + 1) * sub, cols])
                    s[t, ri] = piece
                    m_piece = jnp.max(piece, axis=0, keepdims=True)
                    if exists[t] is not None:
                        m_piece = jnp.where(exists[t], m_piece, -jnp.inf)
                    m = m_piece if m is None else jnp.maximum(m, m_piece)
            for t in range(n_tiles):
                off = m if exists[t] is None else jnp.where(exists[t], m, jnp.inf)
                for ri in range(n_sub):
                    if live[t][ri][ci]:
                        p_pieces[t][ri][ci] = jnp.exp((s[t, ri] - off).astype(BF16))
                    else:
                        p_pieces[t][ri][ci] = jnp.zeros((sub, sub), BF16)
        acc = None
        for t in range(n_tiles):
            p = jnp.concatenate([jnp.concatenate(row, axis=1) for row in p_pieces[t]], axis=0)
            lhs = jnp.concatenate([v_refs[t][h * dh:(h + 1) * dh, :], ones], axis=0)
            part = _dot(lhs, p)
            acc = part if acc is None else acc + part
        ot_ref[h * dh:(h + 1) * dh, :] = (acc[:dh] * (1.0 / acc[dh:dh + 1])).astype(BF16)

    n_pairs = heads // 2
    s_next = pair_scores(0)
    for hp in range(n_pairs):
        s_cur = s_next
        if hp + 1 < n_pairs:
            s_next = pair_scores(hp + 1)
        head_output(2 * hp, s_cur)
        head_output(2 * hp + 1, s_cur)
    o_ref[...] = x_ref[...] + _dot_tn(ot_ref[...], wout_ref[...])


def _att_core(qt, k, vt, x2, rel_rows, wout_bf16, *, batch, seq):
    tokens, d = x2.shape
    blk = SEQ_BLOCK
    nblk = seq // blk
    dh = d // ATT_HEADS

    def k_map(back):
        return lambda b, j: (b * nblk + jnp.maximum(j - back, 0), 0)

    def v_map(back):
        return lambda b, j: (0, b * nblk + jnp.maximum(j - back, 0))

    row = lambda b, j: (b * nblk + j, 0)
    return pl.pallas_call(
        functools.partial(_att_core_kernel, heads=ATT_HEADS, dh=dh),
        grid=(batch, nblk),
        in_specs=[
            pl.BlockSpec((d, blk), lambda b, j: (0, b * nblk + j)),
            pl.BlockSpec((blk, d), k_map(2)),
            pl.BlockSpec((blk, d), k_map(1)),
            pl.BlockSpec((blk, d), k_map(0)),
            pl.BlockSpec((d, blk), v_map(2)),
            pl.BlockSpec((d, blk), v_map(1)),
            pl.BlockSpec((d, blk), v_map(0)),
            pl.BlockSpec((blk, d), row),
            _resident(rel_rows.shape),
            _resident((d, d)),
        ],
        out_specs=pl.BlockSpec((blk, d), row),
        out_shape=jax.ShapeDtypeStruct((tokens, d), F32),
        scratch_shapes=[pltpu.VMEM((d, blk), BF16),
                        pltpu.VMEM((ATT_HEADS, 3 * blk, blk), F32)],
        compiler_params=_params("arbitrary", "arbitrary"),
        name="att_core",
    )(qt, k, k, k, vt, vt, vt, x2, rel_rows, wout_bf16)


def _attention_rel_rows(rel_bias):
    blk = SEQ_BLOCK
    assert PAST_CHUNKS * CHUNK == 2 * blk
    a = np.arange(2 * blk)[None, :]
    t = np.arange(3)[:, None]
    rel = blk * (t - 1) - a
    idx = np.clip(np.maximum(rel, -MAX_REL) + MAX_REL, 0, rel_bias.shape[1] - 1)
    return rel_bias.astype(F32)[:, idx]


def kernel(x, mix_norm_g, ret_w_in, ret_gn_g, ret_w_out, att_w_in, att_rel_bias, att_w_out,
           mlp_norm_g, mlp_w1, mlp_w2, final_norm_g):
    batch, seq, d = x.shape
    depth = mix_norm_g.shape[0]
    assert seq % ROW_BLOCK == 0 and seq % SEQ_BLOCK == 0 and SEQ_BLOCK % CHUNK == 0
    h = x.reshape(batch * seq, d)
    row_vec = lambda v: v.reshape(1, -1).astype(F32)

    for i in range(depth):
        jdx = i // 2
        if i % 2 == 0:
            w_in = ret_w_in[jdx].astype(BF16)
            qk = d
            vw = (w_in.shape[1] - 2 * qk) // 2
            dk, dv = qk // RET_HEADS, vw // RET_HEADS
            cos, sin = _rope_tables(seq, dk)
            q, k, v, gate = _ret_proj(h, row_vec(mix_norm_g[i]), w_in, cos, sin, seq=seq)
            h = _ret_core(q, k, v, gate, h, _retention_tables(dk, dv), row_vec(ret_gn_g[jdx]),
                          ret_w_out[jdx].astype(BF16), batch=batch, seq=seq)
        else:
            w_in = att_w_in[jdx]
            wqt = w_in[:, :d].T.astype(BF16)
            wk = w_in[:, d:2 * d].astype(BF16)
            wvt = w_in[:, 2 * d:].T.astype(BF16)
            qt, k, vt = _att_proj(h, row_vec(mix_norm_g[i]), wqt, wk, wvt)
            h = _att_core(qt, k, vt, h, _attention_rel_rows(att_rel_bias[jdx]),
                          att_w_out[jdx].astype(BF16), batch=batch, seq=seq)
        last = i == depth - 1
        h = _mlp(h, row_vec(mlp_norm_g[i]), mlp_w1[i].astype(BF16), mlp_w2[i].astype(BF16),
                 row_vec(final_norm_g) if last else None)
    if depth == 0:
        raise NotImplementedError("depth 0 has no Pallas stage")
    return h.reshape(batch, seq, d)
```

```python
import functools

import numpy as np
import jax
import jax.numpy as jnp
from jax import lax
from jax.experimental import pallas as pl
from jax.experimental.pallas import tpu as pltpu

CHUNK = 64
RET_HEADS = 4
ATT_HEADS = 16
PAST_CHUNKS = 8
MAX_REL = 256
ROPE_BASE = 10000.0
EPS = 1e-6

V7X_LANES = 128
V7X_VMEM_LIMIT_BYTES = 60000 * 1024

ROW_BLOCK = 512
SEQ_BLOCK = 256
FF_BLOCK = 1024

BF16 = jnp.bfloat16
F32 = jnp.float32


def _params(*semantics):
    return pltpu.CompilerParams(dimension_semantics=semantics,
                                vmem_limit_bytes=V7X_VMEM_LIMIT_BYTES)


def _resident(shape):
    zeros = (0,) * len(shape)
    return pl.BlockSpec(shape, lambda *_: zeros, pipeline_mode=pl.Buffered(1))


def _rmsnorm_bf16(x, g):
    ms = jnp.mean(x * x, axis=-1, keepdims=True)
    return (x * lax.rsqrt(ms + EPS) * g).astype(BF16)


def _dot(a, b):
    return jnp.dot(a, b, preferred_element_type=F32)


def _dot_nt(a, b):
    return lax.dot_general(a, b, (((1,), (1,)), ((), ())), preferred_element_type=F32)


def _dot_tn(a, b):
    return lax.dot_general(a, b, (((0,), (0,)), ((), ())), preferred_element_type=F32)


def _ret_proj_kernel(x_ref, g_ref, w_ref, cos_ref, sin_ref, qdec_ref, kdec_ref, gn_ref,
                     q_ref, k_ref, v_ref, gate_ref, *, heads, dk):
    hn = _rmsnorm_bf16(x_ref[...], g_ref[...])
    cos = cos_ref[...]
    sin = sin_ref[...]
    half = dk // 2
    qk_width = heads * dk

    def rope_store(out_ref, col0, dec_ref):
        for h in range(heads):
            t = _dot(hn, w_ref[:, col0 + h * dk: col0 + (h + 1) * dk])
            t1 = t[:, :half]
            t2 = t[:, half:]
            dec = dec_ref[h]
            out_ref[:, h * dk: h * dk + half] = ((t1 * cos - t2 * sin) * dec).astype(BF16)
            out_ref[:, h * dk + half: (h + 1) * dk] = ((t1 * sin + t2 * cos) * dec).astype(BF16)

    vw = v_ref.shape[1]
    step = 512
    for c in range(0, vw, step):
        gate = _dot(hn, w_ref[:, 2 * qk_width + vw + c: 2 * qk_width + vw + c + step])
        silu2 = gate * (1.0 + jnp.tanh(0.5 * gate))
        gate_ref[:, c:c + step] = (silu2 * (0.5 * gn_ref[:, c:c + step])).astype(BF16)
    rope_store(q_ref, 0, qdec_ref)
    rope_store(k_ref, qk_width, kdec_ref)
    for c in range(0, vw, step):
        v_ref[:, c:c + step] = _dot(hn, w_ref[:, 2 * qk_width + c: 2 * qk_width + c + step]).astype(BF16)


def _ret_proj(x2, g, w_bf16, cos, sin, qdec, kdec, gn_g, *, seq):
    tokens, d = x2.shape
    n_all = w_bf16.shape[1]
    qk = d
    vw = (n_all - 2 * qk) // 2
    dk = qk // RET_HEADS
    tm = ROW_BLOCK
    blocks_per_seq = seq // tm
    row = lambda i: (i, 0)
    pos = lambda i: (i % blocks_per_seq, 0)
    return pl.pallas_call(
        functools.partial(_ret_proj_kernel, heads=RET_HEADS, dk=dk),
        grid=(tokens // tm,),
        in_specs=[
            pl.BlockSpec((tm, d), row),
            _resident((1, d)),
            _resident((d, n_all)),
            pl.BlockSpec((tm, dk // 2), pos),
            pl.BlockSpec((tm, dk // 2), pos),
            _resident(qdec.shape),
            _resident(kdec.shape),
            _resident((1, vw)),
        ],
        out_specs=[
            pl.BlockSpec((tm, qk), row),
            pl.BlockSpec((tm, qk), row),
            pl.BlockSpec((tm, vw), row),
            pl.BlockSpec((tm, vw), row),
        ],
        out_shape=[
            jax.ShapeDtypeStruct((tokens, qk), BF16),
            jax.ShapeDtypeStruct((tokens, qk), BF16),
            jax.ShapeDtypeStruct((tokens, vw), BF16),
            jax.ShapeDtypeStruct((tokens, vw), BF16),
        ],
        compiler_params=_params("parallel"),
        name="ret_proj",
    )(x2, g, w_bf16, cos, sin, qdec, kdec, gn_g)


def _ret_core_kernel(q_ref, k_ref, v_ref, gate_ref, x_ref, mask_ref, wout_ref, o_ref, state_ref,
                     *, heads, dk, dv, block_decay):
    @pl.when(pl.program_id(1) == 0)
    def _():
        state_ref[...] = jnp.zeros_like(state_ref)

    def retention(h):
        qh = q_ref[:, h * dk:(h + 1) * dk]
        kh = k_ref[:, h * dk:(h + 1) * dk]
        vh = v_ref[:, h * dv:(h + 1) * dv]
        state = state_ref[h]
        scores = _dot_nt(qh, kh) * mask_ref[h]
        o = _dot(scores.astype(BF16), vh) + _dot(qh, state.astype(BF16))
        state_ref[h] = state * block_decay[h] + _dot_tn(kh, vh)
        return o

    def normed_gated(h, o):
        mu = jnp.mean(o, axis=-1, keepdims=True)
        cen = o - mu
        var = jnp.mean(cen * cen, axis=-1, keepdims=True)
        y = cen * lax.rsqrt(var + EPS) * gate_ref[:, h * dv:(h + 1) * dv].astype(F32)
        return y.astype(BF16)

    acc = x_ref[...]
    o_next = retention(0)
    for h in range(heads):
        o_cur = o_next
        if h + 1 < heads:
            o_next = retention(h + 1)
        acc = acc + _dot(normed_gated(h, o_cur), wout_ref[h * dv:(h + 1) * dv, :])
    o_ref[...] = acc


def _ret_core(q, k, v, gate, x2, mask, block_decay, wout_bf16, *, batch, seq):
    tokens, d = x2.shape
    qk = q.shape[1]
    vw = v.shape[1]
    dk = qk // RET_HEADS
    dv = vw // RET_HEADS
    blk = SEQ_BLOCK
    nblk = seq // blk
    row = lambda b, j: (b * nblk + j, 0)
    return pl.pallas_call(
        functools.partial(_ret_core_kernel, heads=RET_HEADS, dk=dk, dv=dv, block_decay=block_decay),
        grid=(batch, nblk),
        in_specs=[
            pl.BlockSpec((blk, qk), row),
            pl.BlockSpec((blk, qk), row),
            pl.BlockSpec((blk, vw), row),
            pl.BlockSpec((blk, vw), row),
            pl.BlockSpec((blk, d), row),
            _resident(mask.shape),
            _resident((vw, d)),
        ],
        out_specs=pl.BlockSpec((blk, d), row),
        out_shape=jax.ShapeDtypeStruct((tokens, d), F32),
        scratch_shapes=[pltpu.VMEM((RET_HEADS, dk, dv), F32)],
        compiler_params=_params("parallel", "arbitrary"),
        name="ret_core",
    )(q, k, v, gate, x2, mask, wout_bf16)


def _retention_tables(dk):
    blk = SEQ_BLOCK
    log_gamma = np.log1p(-np.exp2(-5.0 - np.arange(RET_HEADS, dtype=np.float64)))
    n = np.arange(blk)
    diff = n[:, None] - n[None, :]
    same = (n[:, None] // CHUNK) == (n[None, :] // CHUNK)
    earlier = (n[None, :] // CHUNK) < (n[:, None] // CHUNK)
    lg = log_gamma[:, None, None]
    log_mask = np.where(same[None], lg * np.abs(diff)[None], lg * diff[None])
    log_qk = lg * ((n + 1.0)[:, None] + (blk - 1.0 - n)[None, :])[None]
    mask = np.where((same | earlier)[None], np.exp(log_mask - log_qk), 0.0)
    qdec = np.exp(log_gamma[:, None] * (n + 1.0)[None, :])
    kdec = np.exp(log_gamma[:, None] * (blk - 1.0 - n)[None, :]) * dk ** -0.5
    block_decay = tuple(float(c) for c in np.exp(log_gamma * blk))
    reps = ROW_BLOCK // blk
    lanes = dk // 2
    per_row = lambda a: np.broadcast_to(np.tile(a, (1, reps))[:, :, None], (RET_HEADS, ROW_BLOCK, lanes))
    as_f32 = lambda a: jnp.asarray(np.ascontiguousarray(a), dtype=F32)
    return as_f32(mask), as_f32(per_row(qdec)), as_f32(per_row(kdec)), block_decay


def _rope_tables(seq, dk):
    half = dk // 2
    inv = np.exp(-np.log(ROPE_BASE) * np.arange(half, dtype=np.float64) / half)
    ang = np.arange(seq, dtype=np.float64)[:, None] * inv[None, :]
    return jnp.asarray(np.cos(ang), dtype=F32), jnp.asarray(np.sin(ang), dtype=F32)


def _mlp_kernel(x_ref, g_ref, w1_ref, w2_ref, *rest, final_norm):
    if final_norm:
        gf_ref, o_ref = rest
    else:
        (o_ref,) = rest
    x = x_ref[...]
    hn = _rmsnorm_bf16(x, g_ref[...])
    acc = x
    d_ff = w1_ref.shape[1]
    for c in range(0, d_ff, FF_BLOCK):
        u = jnp.maximum(_dot(hn, w1_ref[:, c:c + FF_BLOCK]), 0.0)
        acc = acc + _dot((u * u).astype(BF16), w2_ref[c:c + FF_BLOCK, :])
    if final_norm:
        ms = jnp.mean(acc * acc, axis=-1, keepdims=True)
        acc = acc * lax.rsqrt(ms + EPS) * gf_ref[...]
    o_ref[...] = acc


def _mlp(x2, g, w1_bf16, w2_bf16, final_g=None):
    tokens, d = x2.shape
    d_ff = w1_bf16.shape[1]
    tm = ROW_BLOCK
    row = lambda i: (i, 0)
    in_specs = [pl.BlockSpec((tm, d), row), _resident((1, d)), _resident((d, d_ff)), _resident((d_ff, d))]
    args = [x2, g, w1_bf16, w2_bf16]
    if final_g is not None:
        in_specs.append(_resident((1, d)))
        args.append(final_g)
    return pl.pallas_call(
        functools.partial(_mlp_kernel, final_norm=final_g is not None),
        grid=(tokens // tm,),
        in_specs=in_specs,
        out_specs=pl.BlockSpec((tm, d), row),
        out_shape=jax.ShapeDtypeStruct((tokens, d), F32),
        compiler_params=_params("parallel"),
        name="mlp_final" if final_g is not None else "mlp",
    )(*args)


def _att_proj_kernel(x_ref, g_ref, wqt_ref, wk_ref, wvt_ref, qt_ref, k_ref, vt_ref, *, q_scale):
    hn = _rmsnorm_bf16(x_ref[...], g_ref[...])
    d = x_ref.shape[1]
    n_sub, _, blk = qt_ref.shape
    step = 512

    def store_feature_major(out_ref, c, t):
        for s in range(n_sub):
            out_ref[s, c:c + step, :] = t[:, s * blk:(s + 1) * blk].astype(BF16)

    for c in range(0, d, step):
        store_feature_major(qt_ref, c, _dot_nt(wqt_ref[c:c + step, :], hn) * q_scale)
    for c in range(0, d, step):
        k_ref[:, c:c + step] = _dot(hn, wk_ref[:, c:c + step]).astype(BF16)
    for c in range(0, d, step):
        store_feature_major(vt_ref, c, _dot_nt(wvt_ref[c:c + step, :], hn))


def _att_proj(x2, g, wqt, wk, wvt):
    tokens, d = x2.shape
    tm = ROW_BLOCK
    blk = SEQ_BLOCK
    row = lambda i: (i, 0)
    slab = lambda i: (i, 0, 0)
    return pl.pallas_call(
        functools.partial(_att_proj_kernel, q_scale=(d // ATT_HEADS) ** -0.5),
        grid=(tokens // tm,),
        in_specs=[pl.BlockSpec((tm, d), row), _resident((1, d)),
                  _resident((d, d)), _resident((d, d)), _resident((d, d))],
        out_specs=[pl.BlockSpec((tm // blk, d, blk), slab), pl.BlockSpec((tm, d), row),
                   pl.BlockSpec((tm // blk, d, blk), slab)],
        out_shape=[jax.ShapeDtypeStruct((tokens // blk, d, blk), BF16),
                   jax.ShapeDtypeStruct((tokens, d), BF16),
                   jax.ShapeDtypeStruct((tokens // blk, d, blk), BF16)],
        compiler_params=_params("parallel"),
        name="att_proj",
    )(x2, g, wqt, wk, wvt)


def _band_live_pieces(blk, sub):
    m = np.arange(3 * blk)[:, None]
    n = np.arange(blk)[None, :]
    key_chunk = m // CHUNK - PAST_CHUNKS
    query_chunk = n // CHUNK
    valid = (key_chunk <= query_chunk) & (key_chunk >= query_chunk - PAST_CHUNKS)
    pieces = valid.reshape(3, blk // sub, sub, blk // sub, sub).any(axis=(2, 4))
    return pieces.tolist()


def _att_core_kernel(qt_ref, k0_ref, k1_ref, k2_ref, v0_ref, v1_ref, v2_ref, x_ref, rel_ref,
                     wout_ref, o_ref, ot_ref, bias_ref, *, heads, dh):
    b = pl.program_id(0)
    j = pl.program_id(1)
    blk = qt_ref.shape[1]
    n_tiles = 3

    @pl.when((b == 0) & (j == 0))
    def _build_bias():
        r = lax.broadcasted_iota(jnp.int32, (blk, blk), 0)
        c = lax.broadcasted_iota(jnp.int32, (blk, blk), 1)
        shift = CHUNK.bit_length() - 1
        query_chunk = lax.shift_right_logical(c, shift)
        chunks_per_blk = blk // CHUNK

        def per_head(h, carry):
            for t in range(n_tiles):
                key_chunk = lax.shift_right_logical(r, shift) + (t * chunks_per_blk - PAST_CHUNKS)
                valid = (key_chunk <= query_chunk) & (key_chunk >= query_chunk - PAST_CHUNKS)
                row = jnp.broadcast_to(rel_ref[h, t:t + 1, :], (blk, 2 * blk))
                tile = pltpu.roll(row, blk, 1, stride=1, stride_axis=0)[:, :blk]
                bias_ref[h, t * blk:(t + 1) * blk, :] = jnp.where(valid, tile, -jnp.inf)
            return carry

        lax.fori_loop(0, heads, per_head, 0)

    k_refs = (k0_ref, k1_ref, k2_ref)
    v_refs = (v0_ref, v1_ref, v2_ref)
    exists = (j >= 2, j >= 1, None)
    zeros = jnp.zeros((dh, blk), BF16)
    ones = jnp.ones((16, blk), BF16)

    def pair_scores(hp):
        lo = hp * 2 * dh
        q_even = jnp.concatenate([qt_ref[lo:lo + dh, :], zeros], axis=0)
        q_odd = jnp.concatenate([zeros, qt_ref[lo + dh:lo + 2 * dh, :]], axis=0)
        q_pair = jnp.concatenate([q_even, q_odd], axis=1)
        return [_dot(k_refs[t][:, lo:lo + 2 * dh], q_pair) for t in range(n_tiles)]

    sub = V7X_LANES
    n_sub = blk // sub
    live = _band_live_pieces(blk, sub)

    def head_output(h, s_pair):
        par = h % 2
        p_pieces = [[[None] * n_sub for _ in range(n_sub)] for _ in range(n_tiles)]
        for ci in range(n_sub):
            cols = slice(ci * sub, (ci + 1) * sub)
            s = {}
            m = None
            for t in range(n_tiles):
                for ri in range(n_sub):
                    if not live[t][ri][ci]:
                        continue
                    rows = slice(ri * sub, (ri + 1) * sub)
                    piece = (s_pair[t][rows, par * blk + ci * sub:par * blk + (ci + 1) * sub]
                             + bias_ref[h, t * blk + ri * sub:t * blk + (ri + 1) * sub, cols])
                    s[t, ri] = piece
                    m_piece = jnp.max(piece, axis=0, keepdims=True)
                    if exists[t] is not None:
                        m_piece = jnp.where(exists[t], m_piece, -jnp.inf)
                    m = m_piece if m is None else jnp.maximum(m, m_piece)
            for t in range(n_tiles):
                off = m if exists[t] is None else jnp.where(exists[t], m, jnp.inf)
                for ri in range(n_sub):
                    if live[t][ri][ci]:
                        p_pieces[t][ri][ci] = jnp.exp((s[t, ri] - off).astype(BF16))
                    else:
                        p_pieces[t][ri][ci] = jnp.zeros((sub, sub), BF16)
        acc = None
        for t in range(n_tiles):
            p = jnp.concatenate([jnp.concatenate(row, axis=1) for row in p_pieces[t]], axis=0)
            lhs = jnp.concatenate([v_refs[t][h * dh:(h + 1) * dh, :], ones], axis=0)
            part = _dot(lhs, p)
            acc = part if acc is None else acc + part
        ot_ref[h * dh:(h + 1) * dh, :] = (acc[:dh] * (1.0 / acc[dh:dh + 1])).astype(BF16)

    n_pairs = heads // 2
    s_next = pair_scores(0)
    for hp in range(n_pairs):
        s_cur = s_next
        if hp + 1 < n_pairs:
            s_next = pair_scores(hp + 1)
        head_output(2 * hp, s_cur)
        head_output(2 * hp + 1, s_cur)
    o_ref[...] = x_ref[...] + _dot_tn(ot_ref[...], wout_ref[...])


def _att_core(qt, k, vt, x2, rel_rows, wout_bf16, *, batch, seq):
    tokens, d = x2.shape
    blk = SEQ_BLOCK
    nblk = seq // blk
    dh = d // ATT_HEADS

    def k_map(back):
        return lambda b, j: (b * nblk + jnp.maximum(j - back, 0), 0)

    def v_map(back):
        return lambda b, j: (b * nblk + jnp.maximum(j - back, 0), 0, 0)

    row = lambda b, j: (b * nblk + j, 0)
    return pl.pallas_call(
        functools.partial(_att_core_kernel, heads=ATT_HEADS, dh=dh),
        grid=(batch, nblk),
        in_specs=[
            pl.BlockSpec((None, d, blk), v_map(0)),
            pl.BlockSpec((blk, d), k_map(2)),
            pl.BlockSpec((blk, d), k_map(1)),
            pl.BlockSpec((blk, d), k_map(0)),
            pl.BlockSpec((None, d, blk), v_map(2)),
            pl.BlockSpec((None, d, blk), v_map(1)),
            pl.BlockSpec((None, d, blk), v_map(0)),
            pl.BlockSpec((blk, d), row),
            _resident(rel_rows.shape),
            _resident((d, d)),
        ],
        out_specs=pl.BlockSpec((blk, d), row),
        out_shape=jax.ShapeDtypeStruct((tokens, d), F32),
        scratch_shapes=[pltpu.VMEM((d, blk), BF16),
                        pltpu.VMEM((ATT_HEADS, 3 * blk, blk), F32)],
        compiler_params=_params("arbitrary", "arbitrary"),
        name="att_core",
    )(qt, k, k, k, vt, vt, vt, x2, rel_rows, wout_bf16)


def _attention_rel_rows(rel_bias):
    blk = SEQ_BLOCK
    assert PAST_CHUNKS * CHUNK == 2 * blk
    a = np.arange(2 * blk)[None, :]
    t = np.arange(3)[:, None]
    rel = blk * (t - 1) - a
    idx = np.clip(np.maximum(rel, -MAX_REL) + MAX_REL, 0, rel_bias.shape[1] - 1)
    return rel_bias.astype(F32)[:, idx]


def kernel(x, mix_norm_g, ret_w_in, ret_gn_g, ret_w_out, att_w_in, att_rel_bias, att_w_out,
           mlp_norm_g, mlp_w1, mlp_w2, final_norm_g):
    batch, seq, d = x.shape
    depth = mix_norm_g.shape[0]
    assert seq % ROW_BLOCK == 0 and seq % SEQ_BLOCK == 0 and SEQ_BLOCK % CHUNK == 0
    h = x.reshape(batch * seq, d)
    row_vec = lambda v: v.reshape(1, -1).astype(F32)

    for i in range(depth):
        jdx = i // 2
        if i % 2 == 0:
            w_in = ret_w_in[jdx].astype(BF16)
            dk = d // RET_HEADS
            cos, sin = _rope_tables(seq, dk)
            mask, qdec, kdec, block_decay = _retention_tables(dk)
            q, k, v, gate = _ret_proj(h, row_vec(mix_norm_g[i]), w_in, cos, sin, qdec, kdec,
                                      row_vec(ret_gn_g[jdx]), seq=seq)
            h = _ret_core(q, k, v, gate, h, mask, block_decay, ret_w_out[jdx].astype(BF16),
                          batch=batch, seq=seq)
        else:
            w_in = att_w_in[jdx]
            wqt = w_in[:, :d].T.astype(BF16)
            wk = w_in[:, d:2 * d].astype(BF16)
            wvt = w_in[:, 2 * d:].T.astype(BF16)
            qt, k, vt = _att_proj(h, row_vec(mix_norm_g[i]), wqt, wk, wvt)
            h = _att_core(qt, k, vt, h, _attention_rel_rows(att_rel_bias[jdx]),
                          att_w_out[jdx].astype(BF16), batch=batch, seq=seq)
        last = i == depth - 1
        h = _mlp(h, row_vec(mlp_norm_g[i]), mlp_w1[i].astype(BF16), mlp_w2[i].astype(BF16),
                 row_vec(final_norm_g) if last else None)
    if depth == 0:
        raise NotImplementedError("depth 0 has no Pallas stage")
    return h.reshape(batch, seq, d)
```

```python
import functools

import numpy as np
import jax
import jax.numpy as jnp
from jax import lax
from jax.experimental import pallas as pl
from jax.experimental.pallas import tpu as pltpu

CHUNK = 64
RET_HEADS = 4
ATT_HEADS = 16
PAST_CHUNKS = 8
MAX_REL = 256
ROPE_BASE = 10000.0
EPS = 1e-6

V7X_LANES = 128
V7X_VMEM_LIMIT_BYTES = 60000 * 1024

ROW_BLOCK = 1024
ROW_PARTS = 2
SEQ_BLOCK = 256
SEQ_STEP = 1024
FF_BLOCK = 1024

BF16 = jnp.bfloat16
F32 = jnp.float32


def _params(*semantics):
    return pltpu.CompilerParams(dimension_semantics=semantics,
                                vmem_limit_bytes=V7X_VMEM_LIMIT_BYTES)


def _resident(shape):
    zeros = (0,) * len(shape)
    return pl.BlockSpec(shape, lambda *_: zeros, pipeline_mode=pl.Buffered(1))


def _rmsnorm_bf16(x, g):
    ms = jnp.mean(x * x, axis=-1, keepdims=True)
    return (x * lax.rsqrt(ms + EPS) * g).astype(BF16)


def _dot(a, b):
    return jnp.dot(a, b, preferred_element_type=F32)


def _dot_nt(a, b):
    return lax.dot_general(a, b, (((1,), (1,)), ((), ())), preferred_element_type=F32)


def _dot_tn(a, b):
    return lax.dot_general(a, b, (((0,), (0,)), ((), ())), preferred_element_type=F32)


def _ret_proj_kernel(x_ref, g_ref, w_ref, cos_ref, sin_ref, qdec_ref, kdec_ref, gn_ref,
                     q_ref, k_ref, v_ref, gate_ref, *, heads, dk):
    hn = _rmsnorm_bf16(x_ref[...], g_ref[...])
    cos = cos_ref[...]
    sin = sin_ref[...]
    half = dk // 2
    qk_width = heads * dk

    def rope_store(out_ref, col0, dec_ref):
        for h in range(heads):
            t = _dot(hn, w_ref[:, col0 + h * dk: col0 + (h + 1) * dk])
            t1 = t[:, :half]
            t2 = t[:, half:]
            dec = dec_ref[h]
            out_ref[:, h * dk: h * dk + half] = ((t1 * cos - t2 * sin) * dec).astype(BF16)
            out_ref[:, h * dk + half: (h + 1) * dk] = ((t1 * sin + t2 * cos) * dec).astype(BF16)

    vw = v_ref.shape[1]
    step = 512
    for c in range(0, vw, step):
        gate = _dot(hn, w_ref[:, 2 * qk_width + vw + c: 2 * qk_width + vw + c + step])
        silu2 = gate * (1.0 + jnp.tanh(0.5 * gate))
        gate_ref[:, c:c + step] = (silu2 * (0.5 * gn_ref[:, c:c + step])).astype(BF16)
    rope_store(q_ref, 0, qdec_ref)
    rope_store(k_ref, qk_width, kdec_ref)
    for c in range(0, vw, step):
        v_ref[:, c:c + step] = _dot(hn, w_ref[:, 2 * qk_width + c: 2 * qk_width + c + step]).astype(BF16)


def _ret_proj(x2, g, w_bf16, cos, sin, qdec, kdec, gn_g, *, seq):
    tokens, d = x2.shape
    n_all = w_bf16.shape[1]
    qk = d
    vw = (n_all - 2 * qk) // 2
    dk = qk // RET_HEADS
    tm = ROW_BLOCK
    blocks_per_seq = seq // tm
    row = lambda i: (i, 0)
    pos = lambda i: (i % blocks_per_seq, 0)
    return pl.pallas_call(
        functools.partial(_ret_proj_kernel, heads=RET_HEADS, dk=dk),
        grid=(tokens // tm,),
        in_specs=[
            pl.BlockSpec((tm, d), row),
            _resident((1, d)),
            _resident((d, n_all)),
            pl.BlockSpec((tm, dk // 2), pos),
            pl.BlockSpec((tm, dk // 2), pos),
            _resident(qdec.shape),
            _resident(kdec.shape),
            _resident((1, vw)),
        ],
        out_specs=[
            pl.BlockSpec((tm, qk), row),
            pl.BlockSpec((tm, qk), row),
            pl.BlockSpec((tm, vw), row),
            pl.BlockSpec((tm, vw), row),
        ],
        out_shape=[
            jax.ShapeDtypeStruct((tokens, qk), BF16),
            jax.ShapeDtypeStruct((tokens, qk), BF16),
            jax.ShapeDtypeStruct((tokens, vw), BF16),
            jax.ShapeDtypeStruct((tokens, vw), BF16),
        ],
        compiler_params=_params("parallel"),
        name="ret_proj",
    )(x2, g, w_bf16, cos, sin, qdec, kdec, gn_g)


def _ret_core_kernel(q_ref, k_ref, v_ref, gate_ref, x_ref, mask_ref, wout_ref, o_ref, state_ref,
                     *, heads, dk, dv, blk, block_decay):
    @pl.when(pl.program_id(1) == 0)
    def _():
        state_ref[...] = jnp.zeros_like(state_ref)

    def block(s, carry):
        rows = pl.ds(pl.multiple_of(s * blk, blk), blk)

        def retention(h):
            qh = q_ref[rows, h * dk:(h + 1) * dk]
            kh = k_ref[rows, h * dk:(h + 1) * dk]
            vh = v_ref[rows, h * dv:(h + 1) * dv]
            state = state_ref[h]
            scores = _dot_nt(qh, kh) * mask_ref[h]
            o = _dot(scores.astype(BF16), vh) + _dot(qh, state.astype(BF16))
            state_ref[h] = state * block_decay[h] + _dot_tn(kh, vh)
            return o

        def normed_gated(h, o):
            mu = jnp.mean(o, axis=-1, keepdims=True)
            cen = o - mu
            var = jnp.mean(cen * cen, axis=-1, keepdims=True)
            y = cen * lax.rsqrt(var + EPS) * gate_ref[rows, h * dv:(h + 1) * dv].astype(F32)
            return y.astype(BF16)

        acc = x_ref[rows, :]
        o_next = retention(0)
        for h in range(heads):
            o_cur = o_next
            if h + 1 < heads:
                o_next = retention(h + 1)
            acc = acc + _dot(normed_gated(h, o_cur), wout_ref[h * dv:(h + 1) * dv, :])
        o_ref[rows, :] = acc
        return carry

    lax.fori_loop(0, q_ref.shape[0] // blk, block, 0)


def _ret_core(q, k, v, gate, x2, mask, block_decay, wout_bf16, *, batch, seq):
    tokens, d = x2.shape
    qk = q.shape[1]
    vw = v.shape[1]
    dk = qk // RET_HEADS
    dv = vw // RET_HEADS
    step = SEQ_STEP
    nstep = seq // step
    row = lambda b, j: (b * nstep + j, 0)
    return pl.pallas_call(
        functools.partial(_ret_core_kernel, heads=RET_HEADS, dk=dk, dv=dv, blk=SEQ_BLOCK,
                          block_decay=block_decay),
        grid=(batch, nstep),
        in_specs=[
            pl.BlockSpec((step, qk), row),
            pl.BlockSpec((step, qk), row),
            pl.BlockSpec((step, vw), row),
            pl.BlockSpec((step, vw), row),
            pl.BlockSpec((step, d), row),
            _resident(mask.shape),
            _resident((vw, d)),
        ],
        out_specs=pl.BlockSpec((step, d), row),
        out_shape=jax.ShapeDtypeStruct((tokens, d), F32),
        scratch_shapes=[pltpu.VMEM((RET_HEADS, dk, dv), F32)],
        compiler_params=_params("parallel", "arbitrary"),
        name="ret_core",
    )(q, k, v, gate, x2, mask, wout_bf16)


def _retention_tables(dk):
    blk = SEQ_BLOCK
    log_gamma = np.log1p(-np.exp2(-5.0 - np.arange(RET_HEADS, dtype=np.float64)))
    n = np.arange(blk)
    diff = n[:, None] - n[None, :]
    same = (n[:, None] // CHUNK) == (n[None, :] // CHUNK)
    earlier = (n[None, :] // CHUNK) < (n[:, None] // CHUNK)
    lg = log_gamma[:, None, None]
    log_mask = np.where(same[None], lg * np.abs(diff)[None], lg * diff[None])
    log_qk = lg * ((n + 1.0)[:, None] + (blk - 1.0 - n)[None, :])[None]
    mask = np.where((same | earlier)[None], np.exp(log_mask - log_qk), 0.0)
    qdec = np.exp(log_gamma[:, None] * (n + 1.0)[None, :])
    kdec = np.exp(log_gamma[:, None] * (blk - 1.0 - n)[None, :]) * dk ** -0.5
    block_decay = tuple(float(c) for c in np.exp(log_gamma * blk))
    reps = ROW_BLOCK // blk
    lanes = dk // 2
    per_row = lambda a: np.broadcast_to(np.tile(a, (1, reps))[:, :, None], (RET_HEADS, ROW_BLOCK, lanes))
    as_f32 = lambda a: jnp.asarray(np.ascontiguousarray(a), dtype=F32)
    return as_f32(mask), as_f32(per_row(qdec)), as_f32(per_row(kdec)), block_decay


def _rope_tables(seq, dk):
    half = dk // 2
    inv = np.exp(-np.log(ROPE_BASE) * np.arange(half, dtype=np.float64) / half)
    ang = np.arange(seq, dtype=np.float64)[:, None] * inv[None, :]
    return jnp.asarray(np.cos(ang), dtype=F32), jnp.asarray(np.sin(ang), dtype=F32)


def _mlp_kernel(x_ref, g_ref, w1_ref, w2_ref, *rest, final_norm):
    if final_norm:
        gf_ref, o_ref = rest
    else:
        (o_ref,) = rest
    d_ff = w1_ref.shape[1]
    rows = x_ref.shape[0]
    part = rows // ROW_PARTS
    hns = [_rmsnorm_bf16(x_ref[r:r + part, :], g_ref[...]) for r in range(0, rows, part)]
    for idx, r in enumerate(range(0, rows, part)):
        hn = hns[idx]
        acc = x_ref[r:r + part, :]
        for c in range(0, d_ff, FF_BLOCK):
            u = jnp.maximum(_dot(hn, w1_ref[:, c:c + FF_BLOCK]), 0.0)
            acc = acc + _dot((u * u).astype(BF16), w2_ref[c:c + FF_BLOCK, :])
        if final_norm:
            ms = jnp.mean(acc * acc, axis=-1, keepdims=True)
            acc = acc * lax.rsqrt(ms + EPS) * gf_ref[...]
        o_ref[r:r + part, :] = acc


def _mlp(x2, g, w1_bf16, w2_bf16, final_g=None):
    tokens, d = x2.shape
    d_ff = w1_bf16.shape[1]
    tm = ROW_BLOCK
    row = lambda i: (i, 0)
    in_specs = [pl.BlockSpec((tm, d), row), _resident((1, d)), _resident((d, d_ff)), _resident((d_ff, d))]
    args = [x2, g, w1_bf16, w2_bf16]
    if final_g is not None:
        in_specs.append(_resident((1, d)))
        args.append(final_g)
    return pl.pallas_call(
        functools.partial(_mlp_kernel, final_norm=final_g is not None),
        grid=(tokens // tm,),
        in_specs=in_specs,
        out_specs=pl.BlockSpec((tm, d), row),
        out_shape=jax.ShapeDtypeStruct((tokens, d), F32),
        compiler_params=_params("parallel"),
        name="mlp_final" if final_g is not None else "mlp",
    )(*args)


def _att_proj_kernel(x_ref, g_ref, wqt_ref, wk_ref, wvt_ref, qt_ref, k_ref, vt_ref, *, q_scale):
    d = x_ref.shape[1]
    blk = qt_ref.shape[2]
    step = 512
    rows = x_ref.shape[0]
    part = rows // ROW_PARTS
    per = part // blk
    hns = [_rmsnorm_bf16(x_ref[r:r + part, :], g_ref[...]) for r in range(0, rows, part)]
    for idx, r in enumerate(range(0, rows, part)):
        hn = hns[idx]

        def store_feature_major(out_ref, c, t):
            for s in range(per):
                out_ref[idx * per + s, c:c + step, :] = t[:, s * blk:(s + 1) * blk].astype(BF16)

        for c in range(0, d, step):
            store_feature_major(qt_ref, c, _dot_nt(wqt_ref[c:c + step, :], hn) * q_scale)
        for c in range(0, d, step):
            k_ref[r:r + part, c:c + step] = _dot(hn, wk_ref[:, c:c + step]).astype(BF16)
        for c in range(0, d, step):
            store_feature_major(vt_ref, c, _dot_nt(wvt_ref[c:c + step, :], hn))


def _att_proj(x2, g, wqt, wk, wvt):
    tokens, d = x2.shape
    tm = ROW_BLOCK
    blk = SEQ_BLOCK
    row = lambda i: (i, 0)
    slab = lambda i: (i, 0, 0)
    return pl.pallas_call(
        functools.partial(_att_proj_kernel, q_scale=(d // ATT_HEADS) ** -0.5),
        grid=(tokens // tm,),
        in_specs=[pl.BlockSpec((tm, d), row), _resident((1, d)),
                  _resident((d, d)), _resident((d, d)), _resident((d, d))],
        out_specs=[pl.BlockSpec((tm // blk, d, blk), slab), pl.BlockSpec((tm, d), row),
                   pl.BlockSpec((tm // blk, d, blk), slab)],
        out_shape=[jax.ShapeDtypeStruct((tokens // blk, d, blk), BF16),
                   jax.ShapeDtypeStruct((tokens, d), BF16),
                   jax.ShapeDtypeStruct((tokens // blk, d, blk), BF16)],
        compiler_params=_params("parallel"),
        name="att_proj",
    )(x2, g, wqt, wk, wvt)


def _band_live_pieces(blk, sub):
    m = np.arange(3 * blk)[:, None]
    n = np.arange(blk)[None, :]
    key_chunk = m // CHUNK - PAST_CHUNKS
    query_chunk = n // CHUNK
    valid = (key_chunk <= query_chunk) & (key_chunk >= query_chunk - PAST_CHUNKS)
    pieces = valid.reshape(3, blk // sub, sub, blk // sub, sub).any(axis=(2, 4))
    return pieces.tolist()


def _att_core_kernel(qt_ref, k_ref, vt_ref, x_ref, rel_ref, wout_ref, o_ref, ot_ref, bias_ref,
                     khist_ref, vhist_ref, *, heads, dh):
    b = pl.program_id(0)
    j = pl.program_id(1)
    blocks_per_step, _, blk = qt_ref.shape
    n_tiles = 3

    @pl.when((b == 0) & (j == 0))
    def _build_bias():
        r = lax.broadcasted_iota(jnp.int32, (blk, blk), 0)
        c = lax.broadcasted_iota(jnp.int32, (blk, blk), 1)
        shift = CHUNK.bit_length() - 1
        query_chunk = lax.shift_right_logical(c, shift)
        chunks_per_blk = blk // CHUNK

        def per_head(h, carry):
            for t in range(n_tiles):
                key_chunk = lax.shift_right_logical(r, shift) + (t * chunks_per_blk - PAST_CHUNKS)
                valid = (key_chunk <= query_chunk) & (key_chunk >= query_chunk - PAST_CHUNKS)
                row = jnp.broadcast_to(rel_ref[h, t:t + 1, :], (blk, 2 * blk))
                tile = pltpu.roll(row, blk, 1, stride=1, stride_axis=0)[:, :blk]
                bias_ref[h, t * blk:(t + 1) * blk, :] = jnp.where(valid, tile, -jnp.inf)
            return carry

        lax.fori_loop(0, heads, per_head, 0)

    @pl.when(j == 0)
    def _reset_history():
        khist_ref[...] = jnp.zeros_like(khist_ref)
        vhist_ref[...] = jnp.zeros_like(vhist_ref)

    zeros = jnp.zeros((dh, blk), BF16)
    ones = jnp.ones((16, blk), BF16)
    sub = V7X_LANES
    n_sub = blk // sub
    live = _band_live_pieces(blk, sub)

    def block(s, carry):
        rows = pl.ds(pl.multiple_of(s * blk, blk), blk)
        n = j * blocks_per_step + s
        cur = lax.rem(n, 3)
        older = lax.rem(n + 1, 3)
        newer = lax.rem(n + 2, 3)
        exists = (n >= 2, n >= 1, None)

        def k_tile(t, lo):
            if t == 2:
                return k_ref[rows, lo:lo + 2 * dh]
            return khist_ref[older if t == 0 else newer, :, lo:lo + 2 * dh]

        def v_tile(t, h):
            if t == 2:
                return vt_ref[s, h * dh:(h + 1) * dh, :]
            return vhist_ref[older if t == 0 else newer, h * dh:(h + 1) * dh, :]

        def pair_scores(hp):
            lo = hp * 2 * dh
            q_even = jnp.concatenate([qt_ref[s, lo:lo + dh, :], zeros], axis=0)
            q_odd = jnp.concatenate([zeros, qt_ref[s, lo + dh:lo + 2 * dh, :]], axis=0)
            q_pair = jnp.concatenate([q_even, q_odd], axis=1)
            return [_dot(k_tile(t, lo), q_pair) for t in range(n_tiles)]

        def head_output(h, s_pair):
            par = h % 2
            p_pieces = [[[None] * n_sub for _ in range(n_sub)] for _ in range(n_tiles)]
            for ci in range(n_sub):
                cols = slice(ci * sub, (ci + 1) * sub)
                sc = {}
                m = None
                for t in range(n_tiles):
                    for ri in range(n_sub):
                        if not live[t][ri][ci]:
                            continue
                        prow = slice(ri * sub, (ri + 1) * sub)
                        piece = (s_pair[t][prow, par * blk + ci * sub:par * blk + (ci + 1) * sub]
                                 + bias_ref[h, t * blk + ri * sub:t * blk + (ri + 1) * sub, cols])
                        sc[t, ri] = piece
                        m_piece = jnp.max(piece, axis=0, keepdims=True)
                        if exists[t] is not None:
                            m_piece = jnp.where(exists[t], m_piece, -jnp.inf)
                        m = m_piece if m is None else jnp.maximum(m, m_piece)
                for t in range(n_tiles):
                    off = m if exists[t] is None else jnp.where(exists[t], m, jnp.inf)
                    for ri in range(n_sub):
                        if live[t][ri][ci]:
                            p_pieces[t][ri][ci] = jnp.exp((sc[t, ri] - off).astype(BF16))
                        else:
                            p_pieces[t][ri][ci] = jnp.zeros((sub, sub), BF16)
            acc = None
            for t in range(n_tiles):
                p = jnp.concatenate([jnp.concatenate(prow, axis=1) for prow in p_pieces[t]], axis=0)
                lhs = jnp.concatenate([v_tile(t, h), ones], axis=0)
                part = _dot(lhs, p)
                acc = part if acc is None else acc + part
            ot_ref[h * dh:(h + 1) * dh, :] = (acc[:dh] * (1.0 / acc[dh:dh + 1])).astype(BF16)

        n_pairs = heads // 2
        s_next = pair_scores(0)
        for hp in range(n_pairs):
            s_cur = s_next
            if hp + 1 < n_pairs:
                s_next = pair_scores(hp + 1)
            head_output(2 * hp, s_cur)
            head_output(2 * hp + 1, s_cur)
        o_ref[rows, :] = x_ref[rows, :] + _dot_tn(ot_ref[...], wout_ref[...])
        khist_ref[cur] = k_ref[rows, :]
        vhist_ref[cur] = vt_ref[s]
        return carry

    lax.fori_loop(0, blocks_per_step, block, 0)


def _att_core(qt, k, vt, x2, rel_rows, wout_bf16, *, batch, seq):
    tokens, d = x2.shape
    blk = SEQ_BLOCK
    step = SEQ_STEP
    nstep = seq // step
    dh = d // ATT_HEADS
    row = lambda b, j: (b * nstep + j, 0)
    slab = lambda b, j: (b * nstep + j, 0, 0)
    return pl.pallas_call(
        functools.partial(_att_core_kernel, heads=ATT_HEADS, dh=dh),
        grid=(batch, nstep),
        in_specs=[
            pl.BlockSpec((step // blk, d, blk), slab),
            pl.BlockSpec((step, d), row),
            pl.BlockSpec((step // blk, d, blk), slab),
            pl.BlockSpec((step, d), row),
            _resident(rel_rows.shape),
            _resident((d, d)),
        ],
        out_specs=pl.BlockSpec((step, d), row),
        out_shape=jax.ShapeDtypeStruct((tokens, d), F32),
        scratch_shapes=[pltpu.VMEM((d, blk), BF16),
                        pltpu.VMEM((ATT_HEADS, 3 * blk, blk), F32),
                        pltpu.VMEM((3, blk, d), BF16),
                        pltpu.VMEM((3, d, blk), BF16)],
        compiler_params=_params("arbitrary", "arbitrary"),
        name="att_core",
    )(qt, k, vt, x2, rel_rows, wout_bf16)


def _attention_rel_rows(rel_bias):
    blk = SEQ_BLOCK
    assert PAST_CHUNKS * CHUNK == 2 * blk
    a = np.arange(2 * blk)[None, :]
    t = np.arange(3)[:, None]
    rel = blk * (t - 1) - a
    idx = np.clip(np.maximum(rel, -MAX_REL) + MAX_REL, 0, rel_bias.shape[1] - 1)
    return rel_bias.astype(F32)[:, idx]


def kernel(x, mix_norm_g, ret_w_in, ret_gn_g, ret_w_out, att_w_in, att_rel_bias, att_w_out,
           mlp_norm_g, mlp_w1, mlp_w2, final_norm_g):
    batch, seq, d = x.shape
    depth = mix_norm_g.shape[0]
    assert seq % ROW_BLOCK == 0 and seq % SEQ_STEP == 0
    assert ROW_BLOCK % (ROW_PARTS * SEQ_BLOCK) == 0 and SEQ_STEP % SEQ_BLOCK == 0 and SEQ_BLOCK % CHUNK == 0
    h = x.reshape(batch * seq, d)
    row_vec = lambda v: v.reshape(1, -1).astype(F32)

    for i in range(depth):
        jdx = i // 2
        if i % 2 == 0:
            w_in = ret_w_in[jdx].astype(BF16)
            dk = d // RET_HEADS
            cos, sin = _rope_tables(seq, dk)
            mask, qdec, kdec, block_decay = _retention_tables(dk)
            q, k, v, gate = _ret_proj(h, row_vec(mix_norm_g[i]), w_in, cos, sin, qdec, kdec,
                                      row_vec(ret_gn_g[jdx]), seq=seq)
            h = _ret_core(q, k, v, gate, h, mask, block_decay, ret_w_out[jdx].astype(BF16),
                          batch=batch, seq=seq)
        else:
            w_in = att_w_in[jdx]
            wqt = w_in[:, :d].T.astype(BF16)
            wk = w_in[:, d:2 * d].astype(BF16)
            wvt = w_in[:, 2 * d:].T.astype(BF16)
            qt, k, vt = _att_proj(h, row_vec(mix_norm_g[i]), wqt, wk, wvt)
            h = _att_core(qt, k, vt, h, _attention_rel_rows(att_rel_bias[jdx]),
                          att_w_out[jdx].astype(BF16), batch=batch, seq=seq)
        last = i == depth - 1
        h = _mlp(h, row_vec(mlp_norm_g[i]), mlp_w1[i].astype(BF16), mlp_w2[i].astype(BF16),
                 row_vec(final_norm_g) if last else None)
    if depth == 0:
        raise NotImplementedError("depth 0 has no Pallas stage")
    return h.reshape(batch, seq, d)
```

```python
import functools

import numpy as np
import jax
import jax.numpy as jnp
from jax import lax
from jax.experimental import pallas as pl
from jax.experimental.pallas import tpu as pltpu

CHUNK = 64
RET_HEADS = 4
ATT_HEADS = 16
PAST_CHUNKS = 8
MAX_REL = 256
ROPE_BASE = 10000.0
EPS = 1e-6

V7X_LANES = 128
V7X_VMEM_LIMIT_BYTES = 60000 * 1024

ROW_BLOCK = 1024
ROW_PARTS = 2
SEQ_BLOCK = 256
SEQ_STEP = 1024
FF_BLOCK = 1024

BF16 = jnp.bfloat16
F32 = jnp.float32


def _params(*semantics):
    return pltpu.CompilerParams(dimension_semantics=semantics,
                                vmem_limit_bytes=V7X_VMEM_LIMIT_BYTES)


def _resident(shape):
    zeros = (0,) * len(shape)
    return pl.BlockSpec(shape, lambda *_: zeros, pipeline_mode=pl.Buffered(1))


def _rmsnorm_bf16(x, g):
    ms = jnp.mean(x * x, axis=-1, keepdims=True)
    return (x * lax.rsqrt(ms + EPS) * g).astype(BF16)


def _dot(a, b):
    return jnp.dot(a, b, preferred_element_type=F32)


def _dot_nt(a, b):
    return lax.dot_general(a, b, (((1,), (1,)), ((), ())), preferred_element_type=F32)


def _dot_tn(a, b):
    return lax.dot_general(a, b, (((0,), (0,)), ((), ())), preferred_element_type=F32)


def _ret_proj_kernel(x_ref, g_ref, w_ref, cos_ref, sin_ref, qdec_ref, kdec_ref, gn_ref,
                     q_ref, k_ref, v_ref, gate_ref, *, heads, dk):
    hn = _rmsnorm_bf16(x_ref[...], g_ref[...])
    cos = cos_ref[...]
    sin = sin_ref[...]
    half = dk // 2
    qk_width = heads * dk

    def rope_store(out_ref, col0, dec_ref):
        for h in range(heads):
            t = _dot(hn, w_ref[:, col0 + h * dk: col0 + (h + 1) * dk])
            t1 = t[:, :half]
            t2 = t[:, half:]
            dec = dec_ref[h]
            out_ref[:, h * dk: h * dk + half] = ((t1 * cos - t2 * sin) * dec).astype(BF16)
            out_ref[:, h * dk + half: (h + 1) * dk] = ((t1 * sin + t2 * cos) * dec).astype(BF16)

    vw = v_ref.shape[1]
    step = 512
    for c in range(0, vw, step):
        gate = _dot(hn, w_ref[:, 2 * qk_width + vw + c: 2 * qk_width + vw + c + step])
        silu2 = gate * (1.0 + jnp.tanh(0.5 * gate))
        gate_ref[:, c:c + step] = (silu2 * (0.5 * gn_ref[:, c:c + step])).astype(BF16)
    rope_store(q_ref, 0, qdec_ref)
    rope_store(k_ref, qk_width, kdec_ref)
    for c in range(0, vw, step):
        v_ref[:, c:c + step] = _dot(hn, w_ref[:, 2 * qk_width + c: 2 * qk_width + c + step]).astype(BF16)


def _ret_proj(x2, g, w_bf16, cos, sin, qdec, kdec, gn_g, *, seq):
    tokens, d = x2.shape
    n_all = w_bf16.shape[1]
    qk = d
    vw = (n_all - 2 * qk) // 2
    dk = qk // RET_HEADS
    tm = ROW_BLOCK
    blocks_per_seq = seq // tm
    row = lambda i: (i, 0)
    pos = lambda i: (i % blocks_per_seq, 0)
    return pl.pallas_call(
        functools.partial(_ret_proj_kernel, heads=RET_HEADS, dk=dk),
        grid=(tokens // tm,),
        in_specs=[
            pl.BlockSpec((tm, d), row),
            _resident((1, d)),
            _resident((d, n_all)),
            pl.BlockSpec((tm, dk // 2), pos),
            pl.BlockSpec((tm, dk // 2), pos),
            _resident(qdec.shape),
            _resident(kdec.shape),
            _resident((1, vw)),
        ],
        out_specs=[
            pl.BlockSpec((tm, qk), row),
            pl.BlockSpec((tm, qk), row),
            pl.BlockSpec((tm, vw), row),
            pl.BlockSpec((tm, vw), row),
        ],
        out_shape=[
            jax.ShapeDtypeStruct((tokens, qk), BF16),
            jax.ShapeDtypeStruct((tokens, qk), BF16),
            jax.ShapeDtypeStruct((tokens, vw), BF16),
            jax.ShapeDtypeStruct((tokens, vw), BF16),
        ],
        compiler_params=_params("parallel"),
        name="ret_proj",
    )(x2, g, w_bf16, cos, sin, qdec, kdec, gn_g)


def _ret_core_kernel(q_ref, k_ref, v_ref, gate_ref, mask_ref, wout_ref, o_ref, state_ref,
                     *, heads, dk, dv, blk, block_decay):
    @pl.when(pl.program_id(1) == 0)
    def _():
        state_ref[...] = jnp.zeros_like(state_ref)

    def block(s, carry):
        rows = pl.ds(pl.multiple_of(s * blk, blk), blk)

        def retention(h):
            qh = q_ref[rows, h * dk:(h + 1) * dk]
            kh = k_ref[rows, h * dk:(h + 1) * dk]
            vh = v_ref[rows, h * dv:(h + 1) * dv]
            state = state_ref[h]
            scores = _dot_nt(qh, kh) * mask_ref[h]
            o = _dot(scores.astype(BF16), vh) + _dot(qh, state.astype(BF16))
            state_ref[h] = state * block_decay[h] + _dot_tn(kh, vh)
            return o

        def normed_gated(h, o):
            mu = jnp.mean(o, axis=-1, keepdims=True)
            cen = o - mu
            var = jnp.mean(cen * cen, axis=-1, keepdims=True)
            y = cen * lax.rsqrt(var + EPS) * gate_ref[rows, h * dv:(h + 1) * dv].astype(F32)
            return y.astype(BF16)

        acc = None
        o_next = retention(0)
        for h in range(heads):
            o_cur = o_next
            if h + 1 < heads:
                o_next = retention(h + 1)
            part = _dot(normed_gated(h, o_cur), wout_ref[h * dv:(h + 1) * dv, :])
            acc = part if acc is None else acc + part
        o_ref[rows, :] = acc
        return carry

    lax.fori_loop(0, q_ref.shape[0] // blk, block, 0)


def _ret_core(q, k, v, gate, mask, block_decay, wout_bf16, *, batch, seq):
    tokens, qk = q.shape
    d = wout_bf16.shape[1]
    vw = v.shape[1]
    dk = qk // RET_HEADS
    dv = vw // RET_HEADS
    step = SEQ_STEP
    nstep = seq // step
    row = lambda b, j: (b * nstep + j, 0)
    return pl.pallas_call(
        functools.partial(_ret_core_kernel, heads=RET_HEADS, dk=dk, dv=dv, blk=SEQ_BLOCK,
                          block_decay=block_decay),
        grid=(batch, nstep),
        in_specs=[
            pl.BlockSpec((step, qk), row),
            pl.BlockSpec((step, qk), row),
            pl.BlockSpec((step, vw), row),
            pl.BlockSpec((step, vw), row),
            _resident(mask.shape),
            _resident((vw, d)),
        ],
        out_specs=pl.BlockSpec((step, d), row),
        out_shape=jax.ShapeDtypeStruct((tokens, d), F32),
        scratch_shapes=[pltpu.VMEM((RET_HEADS, dk, dv), F32)],
        compiler_params=_params("parallel", "arbitrary"),
        name="ret_core",
    )(q, k, v, gate, mask, wout_bf16)


def _retention_tables(dk):
    blk = SEQ_BLOCK
    log_gamma = np.log1p(-np.exp2(-5.0 - np.arange(RET_HEADS, dtype=np.float64)))
    n = np.arange(blk)
    diff = n[:, None] - n[None, :]
    same = (n[:, None] // CHUNK) == (n[None, :] // CHUNK)
    earlier = (n[None, :] // CHUNK) < (n[:, None] // CHUNK)
    lg = log_gamma[:, None, None]
    log_mask = np.where(same[None], lg * np.abs(diff)[None], lg * diff[None])
    log_qk = lg * ((n + 1.0)[:, None] + (blk - 1.0 - n)[None, :])[None]
    mask = np.where((same | earlier)[None], np.exp(log_mask - log_qk), 0.0)
    qdec = np.exp(log_gamma[:, None] * (n + 1.0)[None, :])
    kdec = np.exp(log_gamma[:, None] * (blk - 1.0 - n)[None, :]) * dk ** -0.5
    block_decay = tuple(float(c) for c in np.exp(log_gamma * blk))
    reps = ROW_BLOCK // blk
    lanes = dk // 2
    per_row = lambda a: np.broadcast_to(np.tile(a, (1, reps))[:, :, None], (RET_HEADS, ROW_BLOCK, lanes))
    as_f32 = lambda a: jnp.asarray(np.ascontiguousarray(a), dtype=F32)
    return as_f32(mask), as_f32(per_row(qdec)), as_f32(per_row(kdec)), block_decay


def _rope_tables(seq, dk):
    half = dk // 2
    inv = np.exp(-np.log(ROPE_BASE) * np.arange(half, dtype=np.float64) / half)
    ang = np.arange(seq, dtype=np.float64)[:, None] * inv[None, :]
    return jnp.asarray(np.cos(ang), dtype=F32), jnp.asarray(np.sin(ang), dtype=F32)


def _mlp_kernel(x_ref, delta_ref, g_ref, w1_ref, w2_ref, *rest, final_norm):
    if final_norm:
        gf_ref, o_ref = rest
    else:
        (o_ref,) = rest
    d_ff = w1_ref.shape[1]
    rows = x_ref.shape[0]
    part = rows // ROW_PARTS
    hs = [x_ref[r:r + part, :] + delta_ref[r:r + part, :] for r in range(0, rows, part)]
    hns = [_rmsnorm_bf16(h, g_ref[...]) for h in hs]
    for idx, r in enumerate(range(0, rows, part)):
        hn = hns[idx]
        acc = hs[idx]
        for c in range(0, d_ff, FF_BLOCK):
            u = jnp.maximum(_dot(hn, w1_ref[:, c:c + FF_BLOCK]), 0.0)
            acc = acc + _dot((u * u).astype(BF16), w2_ref[c:c + FF_BLOCK, :])
        if final_norm:
            ms = jnp.mean(acc * acc, axis=-1, keepdims=True)
            acc = acc * lax.rsqrt(ms + EPS) * gf_ref[...]
        o_ref[r:r + part, :] = acc


def _mlp(x2, delta, g, w1_bf16, w2_bf16, final_g=None):
    tokens, d = x2.shape
    d_ff = w1_bf16.shape[1]
    tm = ROW_BLOCK
    row = lambda i: (i, 0)
    in_specs = [pl.BlockSpec((tm, d), row), pl.BlockSpec((tm, d), row),
                _resident((1, d)), _resident((d, d_ff)), _resident((d_ff, d))]
    args = [x2, delta, g, w1_bf16, w2_bf16]
    if final_g is not None:
        in_specs.append(_resident((1, d)))
        args.append(final_g)
    return pl.pallas_call(
        functools.partial(_mlp_kernel, final_norm=final_g is not None),
        grid=(tokens // tm,),
        in_specs=in_specs,
        out_specs=pl.BlockSpec((tm, d), row),
        out_shape=jax.ShapeDtypeStruct((tokens, d), F32),
        compiler_params=_params("parallel"),
        name="mlp_final" if final_g is not None else "mlp",
    )(*args)


def _att_proj_kernel(x_ref, g_ref, wqt_ref, wk_ref, wvt_ref, qt_ref, k_ref, vt_ref, *, q_scale):
    d = x_ref.shape[1]
    blk = qt_ref.shape[2]
    step = 512
    rows = x_ref.shape[0]
    part = rows // ROW_PARTS
    per = part // blk
    hns = [_rmsnorm_bf16(x_ref[r:r + part, :], g_ref[...]) for r in range(0, rows, part)]
    for idx, r in enumerate(range(0, rows, part)):
        hn = hns[idx]

        def store_feature_major(out_ref, c, t):
            for s in range(per):
                out_ref[idx * per + s, c:c + step, :] = t[:, s * blk:(s + 1) * blk].astype(BF16)

        for c in range(0, d, step):
            store_feature_major(qt_ref, c, _dot_nt(wqt_ref[c:c + step, :], hn) * q_scale)
        for c in range(0, d, step):
            k_ref[r:r + part, c:c + step] = _dot(hn, wk_ref[:, c:c + step]).astype(BF16)
        for c in range(0, d, step):
            store_feature_major(vt_ref, c, _dot_nt(wvt_ref[c:c + step, :], hn))


def _att_proj(x2, g, wqt, wk, wvt):
    tokens, d = x2.shape
    tm = ROW_BLOCK
    blk = SEQ_BLOCK
    row = lambda i: (i, 0)
    slab = lambda i: (i, 0, 0)
    return pl.pallas_call(
        functools.partial(_att_proj_kernel, q_scale=(d // ATT_HEADS) ** -0.5),
        grid=(tokens // tm,),
        in_specs=[pl.BlockSpec((tm, d), row), _resident((1, d)),
                  _resident((d, d)), _resident((d, d)), _resident((d, d))],
        out_specs=[pl.BlockSpec((tm // blk, d, blk), slab), pl.BlockSpec((tm, d), row),
                   pl.BlockSpec((tm // blk, d, blk), slab)],
        out_shape=[jax.ShapeDtypeStruct((tokens // blk, d, blk), BF16),
                   jax.ShapeDtypeStruct((tokens, d), BF16),
                   jax.ShapeDtypeStruct((tokens // blk, d, blk), BF16)],
        compiler_params=_params("parallel"),
        name="att_proj",
    )(x2, g, wqt, wk, wvt)


def _band_live_pieces(blk, sub):
    m = np.arange(3 * blk)[:, None]
    n = np.arange(blk)[None, :]
    key_chunk = m // CHUNK - PAST_CHUNKS
    query_chunk = n // CHUNK
    valid = (key_chunk <= query_chunk) & (key_chunk >= query_chunk - PAST_CHUNKS)
    pieces = valid.reshape(3, blk // sub, sub, blk // sub, sub).any(axis=(2, 4))
    return pieces.tolist()


def _att_core_kernel(qt_ref, k_ref, vt_ref, rel_ref, wout_ref, o_ref, ot_ref, bias_ref,
                     khist_ref, vhist_ref, *, heads, dh):
    b = pl.program_id(0)
    j = pl.program_id(1)
    blocks_per_step, _, blk = qt_ref.shape
    n_tiles = 3

    @pl.when((b == 0) & (j == 0))
    def _build_bias():
        r = lax.broadcasted_iota(jnp.int32, (blk, blk), 0)
        c = lax.broadcasted_iota(jnp.int32, (blk, blk), 1)
        shift = CHUNK.bit_length() - 1
        query_chunk = lax.shift_right_logical(c, shift)
        chunks_per_blk = blk // CHUNK

        def per_head(h, carry):
            for t in range(n_tiles):
                key_chunk = lax.shift_right_logical(r, shift) + (t * chunks_per_blk - PAST_CHUNKS)
                valid = (key_chunk <= query_chunk) & (key_chunk >= query_chunk - PAST_CHUNKS)
                row = jnp.broadcast_to(rel_ref[h, t:t + 1, :], (blk, 2 * blk))
                tile = pltpu.roll(row, blk, 1, stride=1, stride_axis=0)[:, :blk]
                bias_ref[h, t * blk:(t + 1) * blk, :] = jnp.where(valid, tile, -jnp.inf)
            return carry

        lax.fori_loop(0, heads, per_head, 0)

    @pl.when(j == 0)
    def _reset_history():
        khist_ref[...] = jnp.zeros_like(khist_ref)
        vhist_ref[...] = jnp.zeros_like(vhist_ref)

    zeros = jnp.zeros((dh, blk), BF16)
    ones = jnp.ones((16, blk), BF16)
    sub = V7X_LANES
    n_sub = blk // sub
    live = _band_live_pieces(blk, sub)

    def block(s, carry):
        rows = pl.ds(pl.multiple_of(s * blk, blk), blk)
        n = j * blocks_per_step + s
        cur = lax.rem(n, 3)
        older = lax.rem(n + 1, 3)
        newer = lax.rem(n + 2, 3)
        exists = (n >= 2, n >= 1, None)

        def k_tile(t, lo):
            if t == 2:
                return k_ref[rows, lo:lo + 2 * dh]
            return khist_ref[older if t == 0 else newer, :, lo:lo + 2 * dh]

        def v_tile(t, h):
            if t == 2:
                return vt_ref[s, h * dh:(h + 1) * dh, :]
            return vhist_ref[older if t == 0 else newer, h * dh:(h + 1) * dh, :]

        def pair_scores(hp):
            lo = hp * 2 * dh
            q_even = jnp.concatenate([qt_ref[s, lo:lo + dh, :], zeros], axis=0)
            q_odd = jnp.concatenate([zeros, qt_ref[s, lo + dh:lo + 2 * dh, :]], axis=0)
            q_pair = jnp.concatenate([q_even, q_odd], axis=1)
            k_all = jnp.concatenate([k_tile(t, lo) for t in range(n_tiles)], axis=0)
            s_all = _dot(k_all, q_pair)
            return [s_all[t * blk:(t + 1) * blk] for t in range(n_tiles)]

        def head_output(h, s_pair):
            par = h % 2
            p_pieces = [[[None] * n_sub for _ in range(n_sub)] for _ in range(n_tiles)]
            for ci in range(n_sub):
                cols = slice(ci * sub, (ci + 1) * sub)
                sc = {}
                m = None
                for t in range(n_tiles):
                    for ri in range(n_sub):
                        if not live[t][ri][ci]:
                            continue
                        prow = slice(ri * sub, (ri + 1) * sub)
                        piece = (s_pair[t][prow, par * blk + ci * sub:par * blk + (ci + 1) * sub]
                                 + bias_ref[h, t * blk + ri * sub:t * blk + (ri + 1) * sub, cols])
                        sc[t, ri] = piece
                        m_piece = jnp.max(piece, axis=0, keepdims=True)
                        if exists[t] is not None:
                            m_piece = jnp.where(exists[t], m_piece, -jnp.inf)
                        m = m_piece if m is None else jnp.maximum(m, m_piece)
                for t in range(n_tiles):
                    off = m if exists[t] is None else jnp.where(exists[t], m, jnp.inf)
                    for ri in range(n_sub):
                        if live[t][ri][ci]:
                            p_pieces[t][ri][ci] = jnp.exp((sc[t, ri] - off).astype(BF16))
                        else:
                            p_pieces[t][ri][ci] = jnp.zeros((sub, sub), BF16)
            p_all = jnp.concatenate([jnp.concatenate(prow, axis=1)
                                     for t in range(n_tiles) for prow in p_pieces[t]], axis=0)
            lhs_all = jnp.concatenate(
                [jnp.concatenate([v_tile(t, h), ones], axis=0) for t in range(n_tiles)], axis=1)
            acc = _dot(lhs_all, p_all)
            ot_ref[h * dh:(h + 1) * dh, :] = (acc[:dh] * (1.0 / acc[dh:dh + 1])).astype(BF16)

        n_pairs = heads // 2
        s_next = pair_scores(0)
        for hp in range(n_pairs):
            s_cur = s_next
            head_output(2 * hp, s_cur)
            if hp + 1 < n_pairs:
                s_next = pair_scores(hp + 1)
            head_output(2 * hp + 1, s_cur)
        o_ref[rows, :] = _dot_tn(ot_ref[...], wout_ref[...])
        khist_ref[cur] = k_ref[rows, :]
        vhist_ref[cur] = vt_ref[s]
        return carry

    lax.fori_loop(0, blocks_per_step, block, 0)


def _att_core(qt, k, vt, rel_rows, wout_bf16, *, batch, seq):
    tokens, d = k.shape
    blk = SEQ_BLOCK
    step = SEQ_STEP
    nstep = seq // step
    dh = d // ATT_HEADS
    row = lambda b, j: (b * nstep + j, 0)
    slab = lambda b, j: (b * nstep + j, 0, 0)
    return pl.pallas_call(
        functools.partial(_att_core_kernel, heads=ATT_HEADS, dh=dh),
        grid=(batch, nstep),
        in_specs=[
            pl.BlockSpec((step // blk, d, blk), slab),
            pl.BlockSpec((step, d), row),
            pl.BlockSpec((step // blk, d, blk), slab),
            _resident(rel_rows.shape),
            _resident((d, d)),
        ],
        out_specs=pl.BlockSpec((step, d), row),
        out_shape=jax.ShapeDtypeStruct((tokens, d), F32),
        scratch_shapes=[pltpu.VMEM((d, blk), BF16),
                        pltpu.VMEM((ATT_HEADS, 3 * blk, blk), F32),
                        pltpu.VMEM((3, blk, d), BF16),
                        pltpu.VMEM((3, d, blk), BF16)],
        compiler_params=_params("arbitrary", "arbitrary"),
        name="att_core",
    )(qt, k, vt, rel_rows, wout_bf16)


def _attention_rel_rows(rel_bias):
    blk = SEQ_BLOCK
    assert PAST_CHUNKS * CHUNK == 2 * blk
    a = np.arange(2 * blk)[None, :]
    t = np.arange(3)[:, None]
    rel = blk * (t - 1) - a
    idx = np.clip(np.maximum(rel, -MAX_REL) + MAX_REL, 0, rel_bias.shape[1] - 1)
    return rel_bias.astype(F32)[:, idx]


def kernel(x, mix_norm_g, ret_w_in, ret_gn_g, ret_w_out, att_w_in, att_rel_bias, att_w_out,
           mlp_norm_g, mlp_w1, mlp_w2, final_norm_g):
    batch, seq, d = x.shape
    depth = mix_norm_g.shape[0]
    assert seq % ROW_BLOCK == 0 and seq % SEQ_STEP == 0
    assert ROW_BLOCK % (ROW_PARTS * SEQ_BLOCK) == 0 and SEQ_STEP % SEQ_BLOCK == 0 and SEQ_BLOCK % CHUNK == 0
    h = x.reshape(batch * seq, d)
    row_vec = lambda v: v.reshape(1, -1).astype(F32)

    for i in range(depth):
        jdx = i // 2
        if i % 2 == 0:
            w_in = ret_w_in[jdx].astype(BF16)
            dk = d // RET_HEADS
            cos, sin = _rope_tables(seq, dk)
            mask, qdec, kdec, block_decay = _retention_tables(dk)
            q, k, v, gate = _ret_proj(h, row_vec(mix_norm_g[i]), w_in, cos, sin, qdec, kdec,
                                      row_vec(ret_gn_g[jdx]), seq=seq)
            delta = _ret_core(q, k, v, gate, mask, block_decay, ret_w_out[jdx].astype(BF16),
                              batch=batch, seq=seq)
        else:
            w_in = att_w_in[jdx]
            wqt = w_in[:, :d].T.astype(BF16)
            wk = w_in[:, d:2 * d].astype(BF16)
            wvt = w_in[:, 2 * d:].T.astype(BF16)
            qt, k, vt = _att_proj(h, row_vec(mix_norm_g[i]), wqt, wk, wvt)
            delta = _att_core(qt, k, vt, _attention_rel_rows(att_rel_bias[jdx]),
                              att_w_out[jdx].astype(BF16), batch=batch, seq=seq)
        last = i == depth - 1
        h = _mlp(h, delta, row_vec(mlp_norm_g[i]), mlp_w1[i].astype(BF16), mlp_w2[i].astype(BF16),
                 row_vec(final_norm_g) if last else None)
    if depth == 0:
        raise NotImplementedError("depth 0 has no Pallas stage")
    return h.reshape(batch, seq, d)
```

```python
import functools

import numpy as np
import jax
import jax.numpy as jnp
from jax import lax
from jax.experimental import pallas as pl
from jax.experimental.pallas import tpu as pltpu

CHUNK = 64
RET_HEADS = 4
ATT_HEADS = 16
PAST_CHUNKS = 8
MAX_REL = 256
ROPE_BASE = 10000.0
EPS = 1e-6

V7X_LANES = 128
V7X_VMEM_LIMIT_BYTES = 60000 * 1024

ROW_BLOCK = 1024
ROW_PARTS = 4
SEQ_BLOCK = 256
SEQ_STEP = 1024
FF_BLOCK = 1024

BF16 = jnp.bfloat16
F32 = jnp.float32


def _params(*semantics):
    return pltpu.CompilerParams(dimension_semantics=semantics,
                                vmem_limit_bytes=V7X_VMEM_LIMIT_BYTES)


def _resident(shape):
    zeros = (0,) * len(shape)
    return pl.BlockSpec(shape, lambda *_: zeros, pipeline_mode=pl.Buffered(1))


def _rmsnorm_bf16(x, g):
    ms = jnp.mean(x * x, axis=-1, keepdims=True)
    return (x * lax.rsqrt(ms + EPS) * g).astype(BF16)


def _dot(a, b):
    return jnp.dot(a, b, preferred_element_type=F32)


def _dot_nt(a, b):
    return lax.dot_general(a, b, (((1,), (1,)), ((), ())), preferred_element_type=F32)


def _dot_tn(a, b):
    return lax.dot_general(a, b, (((0,), (0,)), ((), ())), preferred_element_type=F32)


def _ret_proj_kernel(x_ref, g_ref, w_ref, cos_ref, sin_ref, qdec_ref, kdec_ref, gn_ref,
                     q_ref, k_ref, v_ref, gate_ref, *, heads, dk):
    hn = _rmsnorm_bf16(x_ref[...], g_ref[...])
    cos = cos_ref[...]
    sin = sin_ref[...]
    half = dk // 2
    qk_width = heads * dk

    def rope_store(out_ref, col0, dec_ref):
        for h in range(heads):
            t = _dot(hn, w_ref[:, col0 + h * dk: col0 + (h + 1) * dk])
            t1 = t[:, :half]
            t2 = t[:, half:]
            dec = dec_ref[h]
            out_ref[:, h * dk: h * dk + half] = ((t1 * cos - t2 * sin) * dec).astype(BF16)
            out_ref[:, h * dk + half: (h + 1) * dk] = ((t1 * sin + t2 * cos) * dec).astype(BF16)

    vw = v_ref.shape[1]
    step = 512
    for c in range(0, vw, step):
        gate = _dot(hn, w_ref[:, 2 * qk_width + vw + c: 2 * qk_width + vw + c + step])
        silu2 = gate * (1.0 + jnp.tanh(0.5 * gate))
        gate_ref[:, c:c + step] = (silu2 * (0.5 * gn_ref[:, c:c + step])).astype(BF16)
    rope_store(q_ref, 0, qdec_ref)
    rope_store(k_ref, qk_width, kdec_ref)
    for c in range(0, vw, step):
        v_ref[:, c:c + step] = _dot(hn, w_ref[:, 2 * qk_width + c: 2 * qk_width + c + step]).astype(BF16)


def _ret_proj(x2, g, w_bf16, cos, sin, qdec, kdec, gn_g, *, seq):
    tokens, d = x2.shape
    n_all = w_bf16.shape[1]
    qk = d
    vw = (n_all - 2 * qk) // 2
    dk = qk // RET_HEADS
    tm = ROW_BLOCK
    blocks_per_seq = seq // tm
    row = lambda i: (i, 0)
    pos = lambda i: (i % blocks_per_seq, 0)
    return pl.pallas_call(
        functools.partial(_ret_proj_kernel, heads=RET_HEADS, dk=dk),
        grid=(tokens // tm,),
        in_specs=[
            pl.BlockSpec((tm, d), row),
            _resident((1, d)),
            _resident((d, n_all)),
            pl.BlockSpec((tm, dk // 2), pos),
            pl.BlockSpec((tm, dk // 2), pos),
            _resident(qdec.shape),
            _resident(kdec.shape),
            _resident((1, vw)),
        ],
        out_specs=[
            pl.BlockSpec((tm, qk), row),
            pl.BlockSpec((tm, qk), row),
            pl.BlockSpec((tm, vw), row),
            pl.BlockSpec((tm, vw), row),
        ],
        out_shape=[
            jax.ShapeDtypeStruct((tokens, qk), BF16),
            jax.ShapeDtypeStruct((tokens, qk), BF16),
            jax.ShapeDtypeStruct((tokens, vw), BF16),
            jax.ShapeDtypeStruct((tokens, vw), BF16),
        ],
        compiler_params=_params("parallel"),
        name="ret_proj",
    )(x2, g, w_bf16, cos, sin, qdec, kdec, gn_g)


def _ret_core_kernel(q_ref, k_ref, v_ref, gate_ref, mask_ref, wout_ref, o_ref, state_ref,
                     *, heads, dk, dv, blk, block_decay):
    @pl.when(pl.program_id(1) == 0)
    def _():
        state_ref[...] = jnp.zeros_like(state_ref)

    def block(s, carry):
        rows = pl.ds(pl.multiple_of(s * blk, blk), blk)

        def retention(h):
            qh = q_ref[rows, h * dk:(h + 1) * dk]
            kh = k_ref[rows, h * dk:(h + 1) * dk]
            vh = v_ref[rows, h * dv:(h + 1) * dv]
            state = state_ref[h]
            scores = _dot_nt(qh, kh) * mask_ref[h]
            o = _dot(scores.astype(BF16), vh) + _dot(qh, state.astype(BF16))
            state_ref[h] = state * block_decay[h] + _dot_tn(kh, vh)
            return o

        def normed_gated(h, o):
            mu = jnp.mean(o, axis=-1, keepdims=True)
            cen = o - mu
            var = jnp.mean(cen * cen, axis=-1, keepdims=True)
            y = cen * lax.rsqrt(var + EPS) * gate_ref[rows, h * dv:(h + 1) * dv].astype(F32)
            return y.astype(BF16)

        acc = None
        o_next = retention(0)
        for h in range(heads):
            o_cur = o_next
            if h + 1 < heads:
                o_next = retention(h + 1)
            part = _dot(normed_gated(h, o_cur), wout_ref[h * dv:(h + 1) * dv, :])
            acc = part if acc is None else acc + part
        o_ref[rows, :] = acc.astype(o_ref.dtype)
        return carry

    lax.fori_loop(0, q_ref.shape[0] // blk, block, 0)


def _ret_core(q, k, v, gate, mask, block_decay, wout_bf16, *, batch, seq):
    tokens, qk = q.shape
    d = wout_bf16.shape[1]
    vw = v.shape[1]
    dk = qk // RET_HEADS
    dv = vw // RET_HEADS
    step = SEQ_STEP
    nstep = seq // step
    row = lambda b, j: (b * nstep + j, 0)
    return pl.pallas_call(
        functools.partial(_ret_core_kernel, heads=RET_HEADS, dk=dk, dv=dv, blk=SEQ_BLOCK,
                          block_decay=block_decay),
        grid=(batch, nstep),
        in_specs=[
            pl.BlockSpec((step, qk), row),
            pl.BlockSpec((step, qk), row),
            pl.BlockSpec((step, vw), row),
            pl.BlockSpec((step, vw), row),
            _resident(mask.shape),
            _resident((vw, d)),
        ],
        out_specs=pl.BlockSpec((step, d), row),
        out_shape=jax.ShapeDtypeStruct((tokens, d), BF16),
        scratch_shapes=[pltpu.VMEM((RET_HEADS, dk, dv), F32)],
        compiler_params=_params("parallel", "arbitrary"),
        name="ret_core",
    )(q, k, v, gate, mask, wout_bf16)


def _retention_tables(dk):
    blk = SEQ_BLOCK
    log_gamma = np.log1p(-np.exp2(-5.0 - np.arange(RET_HEADS, dtype=np.float64)))
    n = np.arange(blk)
    diff = n[:, None] - n[None, :]
    same = (n[:, None] // CHUNK) == (n[None, :] // CHUNK)
    earlier = (n[None, :] // CHUNK) < (n[:, None] // CHUNK)
    lg = log_gamma[:, None, None]
    log_mask = np.where(same[None], lg * np.abs(diff)[None], lg * diff[None])
    log_qk = lg * ((n + 1.0)[:, None] + (blk - 1.0 - n)[None, :])[None]
    mask = np.where((same | earlier)[None], np.exp(log_mask - log_qk), 0.0)
    qdec = np.exp(log_gamma[:, None] * (n + 1.0)[None, :])
    kdec = np.exp(log_gamma[:, None] * (blk - 1.0 - n)[None, :]) * dk ** -0.5
    block_decay = tuple(float(c) for c in np.exp(log_gamma * blk))
    reps = ROW_BLOCK // blk
    lanes = dk // 2
    per_row = lambda a: np.broadcast_to(np.tile(a, (1, reps))[:, :, None], (RET_HEADS, ROW_BLOCK, lanes))
    as_f32 = lambda a: jnp.asarray(np.ascontiguousarray(a), dtype=F32)
    return as_f32(mask), as_f32(per_row(qdec)), as_f32(per_row(kdec)), block_decay


def _rope_tables(seq, dk):
    half = dk // 2
    inv = np.exp(-np.log(ROPE_BASE) * np.arange(half, dtype=np.float64) / half)
    ang = np.arange(seq, dtype=np.float64)[:, None] * inv[None, :]
    return jnp.asarray(np.cos(ang), dtype=F32), jnp.asarray(np.sin(ang), dtype=F32)


def _mlp_kernel(x_ref, delta_ref, g_ref, w1_ref, w2_ref, *rest, final_norm):
    if final_norm:
        gf_ref, o_ref = rest
    else:
        (o_ref,) = rest
    d_ff = w1_ref.shape[1]
    rows = x_ref.shape[0]
    part = rows // ROW_PARTS
    hs = [x_ref[r:r + part, :] + delta_ref[r:r + part, :].astype(F32) for r in range(0, rows, part)]
    hns = [_rmsnorm_bf16(h, g_ref[...]) for h in hs]
    for idx, r in enumerate(range(0, rows, part)):
        hn = hns[idx]
        acc = hs[idx]
        for c in range(0, d_ff, FF_BLOCK):
            u = jnp.maximum(_dot(hn, w1_ref[:, c:c + FF_BLOCK]), 0.0)
            acc = acc + _dot((u * u).astype(BF16), w2_ref[c:c + FF_BLOCK, :])
        if final_norm:
            ms = jnp.mean(acc * acc, axis=-1, keepdims=True)
            acc = acc * lax.rsqrt(ms + EPS) * gf_ref[...]
        o_ref[r:r + part, :] = acc


def _mlp(x2, delta, g, w1_bf16, w2_bf16, final_g=None):
    tokens, d = x2.shape
    d_ff = w1_bf16.shape[1]
    tm = ROW_BLOCK
    row = lambda i: (i, 0)
    in_specs = [pl.BlockSpec((tm, d), row), pl.BlockSpec((tm, d), row),
                _resident((1, d)), _resident((d, d_ff)), _resident((d_ff, d))]
    args = [x2, delta, g, w1_bf16, w2_bf16]
    if final_g is not None:
        in_specs.append(_resident((1, d)))
        args.append(final_g)
    return pl.pallas_call(
        functools.partial(_mlp_kernel, final_norm=final_g is not None),
        grid=(tokens // tm,),
        in_specs=in_specs,
        out_specs=pl.BlockSpec((tm, d), row),
        out_shape=jax.ShapeDtypeStruct((tokens, d), F32),
        compiler_params=_params("parallel"),
        name="mlp_final" if final_g is not None else "mlp",
    )(*args)


def _att_proj_kernel(x_ref, g_ref, wqt_ref, wk_ref, wvt_ref, qt_ref, k_ref, vt_ref, *, q_scale):
    d = x_ref.shape[1]
    blk = qt_ref.shape[2]
    step = 512
    rows = x_ref.shape[0]
    part = rows // ROW_PARTS
    per = part // blk
    hns = [_rmsnorm_bf16(x_ref[r:r + part, :], g_ref[...]) for r in range(0, rows, part)]
    for idx, r in enumerate(range(0, rows, part)):
        hn = hns[idx]

        def store_feature_major(out_ref, c, t):
            for s in range(per):
                out_ref[idx * per + s, c:c + step, :] = t[:, s * blk:(s + 1) * blk].astype(BF16)

        for c in range(0, d, step):
            store_feature_major(qt_ref, c, _dot_nt(wqt_ref[c:c + step, :], hn) * q_scale)
        for c in range(0, d, step):
            k_ref[r:r + part, c:c + step] = _dot(hn, wk_ref[:, c:c + step]).astype(BF16)
        for c in range(0, d, step):
            store_feature_major(vt_ref, c, _dot_nt(wvt_ref[c:c + step, :], hn))


def _att_proj(x2, g, wqt, wk, wvt):
    tokens, d = x2.shape
    tm = ROW_BLOCK
    blk = SEQ_BLOCK
    row = lambda i: (i, 0)
    slab = lambda i: (i, 0, 0)
    return pl.pallas_call(
        functools.partial(_att_proj_kernel, q_scale=(d // ATT_HEADS) ** -0.5),
        grid=(tokens // tm,),
        in_specs=[pl.BlockSpec((tm, d), row), _resident((1, d)),
                  _resident((d, d)), _resident((d, d)), _resident((d, d))],
        out_specs=[pl.BlockSpec((tm // blk, d, blk), slab), pl.BlockSpec((tm, d), row),
                   pl.BlockSpec((tm // blk, d, blk), slab)],
        out_shape=[jax.ShapeDtypeStruct((tokens // blk, d, blk), BF16),
                   jax.ShapeDtypeStruct((tokens, d), BF16),
                   jax.ShapeDtypeStruct((tokens // blk, d, blk), BF16)],
        compiler_params=_params("parallel"),
        name="att_proj",
    )(x2, g, wqt, wk, wvt)


def _band_live_pieces(blk, sub):
    m = np.arange(3 * blk)[:, None]
    n = np.arange(blk)[None, :]
    key_chunk = m // CHUNK - PAST_CHUNKS
    query_chunk = n // CHUNK
    valid = (key_chunk <= query_chunk) & (key_chunk >= query_chunk - PAST_CHUNKS)
    pieces = valid.reshape(3, blk // sub, sub, blk // sub, sub).any(axis=(2, 4))
    return pieces.tolist()


def _att_core_kernel(qt_ref, k_ref, vt_ref, rel_ref, wout_ref, o_ref, ot_ref, bias_ref,
                     khist_ref, vhist_ref, *, heads, dh):
    b = pl.program_id(0)
    j = pl.program_id(1)
    blocks_per_step, _, blk = qt_ref.shape
    n_tiles = 3

    @pl.when((b == 0) & (j == 0))
    def _build_bias():
        r = lax.broadcasted_iota(jnp.int32, (blk, blk), 0)
        c = lax.broadcasted_iota(jnp.int32, (blk, blk), 1)
        shift = CHUNK.bit_length() - 1
        query_chunk = lax.shift_right_logical(c, shift)
        chunks_per_blk = blk // CHUNK

        def per_head(h, carry):
            for t in range(n_tiles):
                key_chunk = lax.shift_right_logical(r, shift) + (t * chunks_per_blk - PAST_CHUNKS)
                valid = (key_chunk <= query_chunk) & (key_chunk >= query_chunk - PAST_CHUNKS)
                row = jnp.broadcast_to(rel_ref[h, t:t + 1, :], (blk, 2 * blk))
                tile = pltpu.roll(row, blk, 1, stride=1, stride_axis=0)[:, :blk]
                bias_ref[h, t * blk:(t + 1) * blk, :] = jnp.where(valid, tile, -jnp.inf)
            return carry

        lax.fori_loop(0, heads, per_head, 0)

    @pl.when(j == 0)
    def _reset_history():
        khist_ref[...] = jnp.zeros_like(khist_ref)
        vhist_ref[...] = jnp.zeros_like(vhist_ref)

    zeros = jnp.zeros((dh, blk), BF16)
    ones = jnp.ones((16, blk), BF16)
    sub = V7X_LANES
    n_sub = blk // sub
    live = _band_live_pieces(blk, sub)

    def block(s, carry):
        rows = pl.ds(pl.multiple_of(s * blk, blk), blk)
        n = j * blocks_per_step + s
        cur = lax.rem(n, 3)
        older = lax.rem(n + 1, 3)
        newer = lax.rem(n + 2, 3)
        exists = (n >= 2, n >= 1, None)

        def k_tile(t, lo):
            if t == 2:
                return k_ref[rows, lo:lo + 2 * dh]
            return khist_ref[older if t == 0 else newer, :, lo:lo + 2 * dh]

        def v_tile(t, h):
            if t == 2:
                return vt_ref[s, h * dh:(h + 1) * dh, :]
            return vhist_ref[older if t == 0 else newer, h * dh:(h + 1) * dh, :]

        def pair_scores(hp):
            lo = hp * 2 * dh
            q_even = jnp.concatenate([qt_ref[s, lo:lo + dh, :], zeros], axis=0)
            q_odd = jnp.concatenate([zeros, qt_ref[s, lo + dh:lo + 2 * dh, :]], axis=0)
            q_pair = jnp.concatenate([q_even, q_odd], axis=1)
            k_all = jnp.concatenate([k_tile(t, lo) for t in range(n_tiles)], axis=0)
            s_all = _dot(k_all, q_pair)
            return [s_all[t * blk:(t + 1) * blk] for t in range(n_tiles)]

        def head_output(h, s_pair):
            par = h % 2
            p_pieces = [[[None] * n_sub for _ in range(n_sub)] for _ in range(n_tiles)]
            for ci in range(n_sub):
                cols = slice(ci * sub, (ci + 1) * sub)
                sc = {}
                m = None
                for t in range(n_tiles):
                    for ri in range(n_sub):
                        if not live[t][ri][ci]:
                            continue
                        prow = slice(ri * sub, (ri + 1) * sub)
                        piece = (s_pair[t][prow, par * blk + ci * sub:par * blk + (ci + 1) * sub]
                                 + bias_ref[h, t * blk + ri * sub:t * blk + (ri + 1) * sub, cols])
                        sc[t, ri] = piece
                        m_piece = jnp.max(piece, axis=0, keepdims=True)
                        if exists[t] is not None:
                            m_piece = jnp.where(exists[t], m_piece, -jnp.inf)
                        m = m_piece if m is None else jnp.maximum(m, m_piece)
                for t in range(n_tiles):
                    off = m if exists[t] is None else jnp.where(exists[t], m, jnp.inf)
                    for ri in range(n_sub):
                        if live[t][ri][ci]:
                            p_pieces[t][ri][ci] = jnp.exp((sc[t, ri] - off).astype(BF16))
                        else:
                            p_pieces[t][ri][ci] = jnp.zeros((sub, sub), BF16)
            p_all = jnp.concatenate([jnp.concatenate(prow, axis=1)
                                     for t in range(n_tiles) for prow in p_pieces[t]], axis=0)
            lhs_all = jnp.concatenate(
                [jnp.concatenate([v_tile(t, h), ones], axis=0) for t in range(n_tiles)], axis=1)
            acc = _dot(lhs_all, p_all)
            ot_ref[h * dh:(h + 1) * dh, :] = (acc[:dh] * (1.0 / acc[dh:dh + 1])).astype(BF16)

        n_pairs = heads // 2
        s_next = pair_scores(0)
        for hp in range(n_pairs):
            s_cur = s_next
            head_output(2 * hp, s_cur)
            if hp + 1 < n_pairs:
                s_next = pair_scores(hp + 1)
            head_output(2 * hp + 1, s_cur)
        o_ref[rows, :] = _dot_tn(ot_ref[...], wout_ref[...]).astype(o_ref.dtype)
        khist_ref[cur] = k_ref[rows, :]
        vhist_ref[cur] = vt_ref[s]
        return carry

    lax.fori_loop(0, blocks_per_step, block, 0)


def _att_core(qt, k, vt, rel_rows, wout_bf16, *, batch, seq):
    tokens, d = k.shape
    blk = SEQ_BLOCK
    step = SEQ_STEP
    nstep = seq // step
    dh = d // ATT_HEADS
    row = lambda b, j: (b * nstep + j, 0)
    slab = lambda b, j: (b * nstep + j, 0, 0)
    return pl.pallas_call(
        functools.partial(_att_core_kernel, heads=ATT_HEADS, dh=dh),
        grid=(batch, nstep),
        in_specs=[
            pl.BlockSpec((step // blk, d, blk), slab),
            pl.BlockSpec((step, d), row),
            pl.BlockSpec((step // blk, d, blk), slab),
            _resident(rel_rows.shape),
            _resident((d, d)),
        ],
        out_specs=pl.BlockSpec((step, d), row),
        out_shape=jax.ShapeDtypeStruct((tokens, d), BF16),
        scratch_shapes=[pltpu.VMEM((d, blk), BF16),
                        pltpu.VMEM((ATT_HEADS, 3 * blk, blk), F32),
                        pltpu.VMEM((3, blk, d), BF16),
                        pltpu.VMEM((3, d, blk), BF16)],
        compiler_params=_params("arbitrary", "arbitrary"),
        name="att_core",
    )(qt, k, vt, rel_rows, wout_bf16)


def _attention_rel_rows(rel_bias):
    blk = SEQ_BLOCK
    assert PAST_CHUNKS * CHUNK == 2 * blk
    a = np.arange(2 * blk)[None, :]
    t = np.arange(3)[:, None]
    rel = blk * (t - 1) - a
    idx = np.clip(np.maximum(rel, -MAX_REL) + MAX_REL, 0, rel_bias.shape[1] - 1)
    return rel_bias.astype(F32)[:, idx]


def kernel(x, mix_norm_g, ret_w_in, ret_gn_g, ret_w_out, att_w_in, att_rel_bias, att_w_out,
           mlp_norm_g, mlp_w1, mlp_w2, final_norm_g):
    batch, seq, d = x.shape
    depth = mix_norm_g.shape[0]
    assert seq % ROW_BLOCK == 0 and seq % SEQ_STEP == 0
    assert ROW_BLOCK % (ROW_PARTS * SEQ_BLOCK) == 0 and SEQ_STEP % SEQ_BLOCK == 0 and SEQ_BLOCK % CHUNK == 0
    h = x.reshape(batch * seq, d)
    row_vec = lambda v: v.reshape(1, -1).astype(F32)

    for i in range(depth):
        jdx = i // 2
        if i % 2 == 0:
            w_in = ret_w_in[jdx].astype(BF16)
            dk = d // RET_HEADS
            cos, sin = _rope_tables(seq, dk)
            mask, qdec, kdec, block_decay = _retention_tables(dk)
            q, k, v, gate = _ret_proj(h, row_vec(mix_norm_g[i]), w_in, cos, sin, qdec, kdec,
                                      row_vec(ret_gn_g[jdx]), seq=seq)
            delta = _ret_core(q, k, v, gate, mask, block_decay, ret_w_out[jdx].astype(BF16),
                              batch=batch, seq=seq)
        else:
            w_in = att_w_in[jdx]
            wqt = w_in[:, :d].T.astype(BF16)
            wk = w_in[:, d:2 * d].astype(BF16)
            wvt = w_in[:, 2 * d:].T.astype(BF16)
            qt, k, vt = _att_proj(h, row_vec(mix_norm_g[i]), wqt, wk, wvt)
            delta = _att_core(qt, k, vt, _attention_rel_rows(att_rel_bias[jdx]),
                              att_w_out[jdx].astype(BF16), batch=batch, seq=seq)
        last = i == depth - 1
        h = _mlp(h, delta, row_vec(mlp_norm_g[i]), mlp_w1[i].astype(BF16), mlp_w2[i].astype(BF16),
                 row_vec(final_norm_g) if last else None)
    if depth == 0:
        raise NotImplementedError("depth 0 has no Pallas stage")
    return h.reshape(batch, seq, d)
```

```python
import functools

import numpy as np
import jax
import jax.numpy as jnp
from jax import lax
from jax.experimental import pallas as pl
from jax.experimental.pallas import tpu as pltpu

CHUNK = 64
RET_HEADS = 4
ATT_HEADS = 16
PAST_CHUNKS = 8
MAX_REL = 256
ROPE_BASE = 10000.0
EPS = 1e-6

V7X_LANES = 128
V7X_VMEM_LIMIT_BYTES = 60000 * 1024

ROW_BLOCK = 1024
ROW_PARTS = 4
SEQ_BLOCK = 256
SEQ_STEP = 1024
FF_BLOCK = 1024

BF16 = jnp.bfloat16
F32 = jnp.float32


def _params(*semantics):
    return pltpu.CompilerParams(dimension_semantics=semantics,
                                vmem_limit_bytes=V7X_VMEM_LIMIT_BYTES)


def _resident(shape):
    zeros = (0,) * len(shape)
    return pl.BlockSpec(shape, lambda *_: zeros, pipeline_mode=pl.Buffered(1))


def _rmsnorm_bf16(x, g):
    ms = jnp.mean(x * x, axis=-1, keepdims=True)
    return (x * lax.rsqrt(ms + EPS) * g).astype(BF16)


def _dot(a, b):
    return jnp.dot(a, b, preferred_element_type=F32)


def _dot_nt(a, b):
    return lax.dot_general(a, b, (((1,), (1,)), ((), ())), preferred_element_type=F32)


def _dot_tn(a, b):
    return lax.dot_general(a, b, (((0,), (0,)), ((), ())), preferred_element_type=F32)


def _ret_proj_kernel(x_ref, g_ref, w_ref, cos_ref, sin_ref, qdec_ref, kdec_ref, gn_ref,
                     q_ref, k_ref, v_ref, gate_ref, *, heads, dk):
    hn = _rmsnorm_bf16(x_ref[...], g_ref[...])
    cos = cos_ref[...]
    sin = sin_ref[...]
    half = dk // 2
    qk_width = heads * dk

    def rope_store(out_ref, col0, dec_ref):
        for h in range(heads):
            t = _dot(hn, w_ref[:, col0 + h * dk: col0 + (h + 1) * dk])
            t1 = t[:, :half]
            t2 = t[:, half:]
            dec = dec_ref[h]
            out_ref[:, h * dk: h * dk + half] = ((t1 * cos - t2 * sin) * dec).astype(BF16)
            out_ref[:, h * dk + half: (h + 1) * dk] = ((t1 * sin + t2 * cos) * dec).astype(BF16)

    vw = v_ref.shape[1]
    step = 512
    for c in range(0, vw, step):
        gate = _dot(hn, w_ref[:, 2 * qk_width + vw + c: 2 * qk_width + vw + c + step])
        silu2 = gate * (1.0 + jnp.tanh(0.5 * gate))
        gate_ref[:, c:c + step] = (silu2 * (0.5 * gn_ref[:, c:c + step])).astype(BF16)
    rope_store(q_ref, 0, qdec_ref)
    rope_store(k_ref, qk_width, kdec_ref)
    for c in range(0, vw, step):
        v_ref[:, c:c + step] = _dot(hn, w_ref[:, 2 * qk_width + c: 2 * qk_width + c + step]).astype(BF16)


def _ret_proj(x2, g, w_bf16, cos, sin, qdec, kdec, gn_g, *, seq):
    tokens, d = x2.shape
    n_all = w_bf16.shape[1]
    qk = d
    vw = (n_all - 2 * qk) // 2
    dk = qk // RET_HEADS
    tm = ROW_BLOCK
    blocks_per_seq = seq // tm
    row = lambda i: (i, 0)
    pos = lambda i: (i % blocks_per_seq, 0)
    return pl.pallas_call(
        functools.partial(_ret_proj_kernel, heads=RET_HEADS, dk=dk),
        grid=(tokens // tm,),
        in_specs=[
            pl.BlockSpec((tm, d), row),
            _resident((1, d)),
            _resident((d, n_all)),
            pl.BlockSpec((tm, dk // 2), pos),
            pl.BlockSpec((tm, dk // 2), pos),
            _resident(qdec.shape),
            _resident(kdec.shape),
            _resident((1, vw)),
        ],
        out_specs=[
            pl.BlockSpec((tm, qk), row),
            pl.BlockSpec((tm, qk), row),
            pl.BlockSpec((tm, vw), row),
            pl.BlockSpec((tm, vw), row),
        ],
        out_shape=[
            jax.ShapeDtypeStruct((tokens, qk), BF16),
            jax.ShapeDtypeStruct((tokens, qk), BF16),
            jax.ShapeDtypeStruct((tokens, vw), BF16),
            jax.ShapeDtypeStruct((tokens, vw), BF16),
        ],
        compiler_params=_params("parallel"),
        name="ret_proj",
    )(x2, g, w_bf16, cos, sin, qdec, kdec, gn_g)


def _ret_core_kernel(q_ref, k_ref, v_ref, gate_ref, mask_ref, wout_ref, o_ref, state_ref,
                     *, heads, dk, dv, blk, block_decay):
    @pl.when(pl.program_id(1) == 0)
    def _():
        state_ref[...] = jnp.zeros_like(state_ref)

    def block(s, carry):
        rows = pl.ds(pl.multiple_of(s * blk, blk), blk)

        def retention(h):
            qh = q_ref[rows, h * dk:(h + 1) * dk]
            kh = k_ref[rows, h * dk:(h + 1) * dk]
            vh = v_ref[rows, h * dv:(h + 1) * dv]
            state = state_ref[h]
            scores = _dot_nt(qh, kh) * mask_ref[h]
            o = _dot(scores.astype(BF16), vh) + _dot(qh, state.astype(BF16))
            state_ref[h] = state * block_decay[h] + _dot_tn(kh, vh)
            return o

        def normed_gated(h, o):
            mu = jnp.mean(o, axis=-1, keepdims=True)
            cen = o - mu
            var = jnp.mean(cen * cen, axis=-1, keepdims=True)
            y = cen * lax.rsqrt(var + EPS) * gate_ref[rows, h * dv:(h + 1) * dv].astype(F32)
            return y.astype(BF16)

        acc = None
        o_next = retention(0)
        for h in range(heads):
            o_cur = o_next
            if h + 1 < heads:
                o_next = retention(h + 1)
            part = _dot(normed_gated(h, o_cur), wout_ref[h * dv:(h + 1) * dv, :])
            acc = part if acc is None else acc + part
        o_ref[rows, :] = acc
        return carry

    lax.fori_loop(0, q_ref.shape[0] // blk, block, 0)


def _ret_core(q, k, v, gate, mask, block_decay, wout_bf16, *, batch, seq):
    tokens, qk = q.shape
    d = wout_bf16.shape[1]
    vw = v.shape[1]
    dk = qk // RET_HEADS
    dv = vw // RET_HEADS
    step = SEQ_STEP
    nstep = seq // step
    row = lambda b, j: (b * nstep + j, 0)
    return pl.pallas_call(
        functools.partial(_ret_core_kernel, heads=RET_HEADS, dk=dk, dv=dv, blk=SEQ_BLOCK,
                          block_decay=block_decay),
        grid=(batch, nstep),
        in_specs=[
            pl.BlockSpec((step, qk), row),
            pl.BlockSpec((step, qk), row),
            pl.BlockSpec((step, vw), row),
            pl.BlockSpec((step, vw), row),
            _resident(mask.shape),
            _resident((vw, d)),
        ],
        out_specs=pl.BlockSpec((step, d), row),
        out_shape=jax.ShapeDtypeStruct((tokens, d), F32),
        scratch_shapes=[pltpu.VMEM((RET_HEADS, dk, dv), F32)],
        compiler_params=_params("parallel", "arbitrary"),
        name="ret_core",
    )(q, k, v, gate, mask, wout_bf16)


def _retention_tables(dk):
    blk = SEQ_BLOCK
    log_gamma = np.log1p(-np.exp2(-5.0 - np.arange(RET_HEADS, dtype=np.float64)))
    n = np.arange(blk)
    diff = n[:, None] - n[None, :]
    same = (n[:, None] // CHUNK) == (n[None, :] // CHUNK)
    earlier = (n[None, :] // CHUNK) < (n[:, None] // CHUNK)
    lg = log_gamma[:, None, None]
    log_mask = np.where(same[None], lg * np.abs(diff)[None], lg * diff[None])
    log_qk = lg * ((n + 1.0)[:, None] + (blk - 1.0 - n)[None, :])[None]
    mask = np.where((same | earlier)[None], np.exp(log_mask - log_qk), 0.0)
    qdec = np.exp(log_gamma[:, None] * (n + 1.0)[None, :])
    kdec = np.exp(log_gamma[:, None] * (blk - 1.0 - n)[None, :]) * dk ** -0.5
    block_decay = tuple(float(c) for c in np.exp(log_gamma * blk))
    reps = ROW_BLOCK // blk
    lanes = dk // 2
    per_row = lambda a: np.broadcast_to(np.tile(a, (1, reps))[:, :, None], (RET_HEADS, ROW_BLOCK, lanes))
    as_f32 = lambda a: jnp.asarray(np.ascontiguousarray(a), dtype=F32)
    return as_f32(mask), as_f32(per_row(qdec)), as_f32(per_row(kdec)), block_decay


def _rope_tables(seq, dk):
    half = dk // 2
    inv = np.exp(-np.log(ROPE_BASE) * np.arange(half, dtype=np.float64) / half)
    ang = np.arange(seq, dtype=np.float64)[:, None] * inv[None, :]
    return jnp.asarray(np.cos(ang), dtype=F32), jnp.asarray(np.sin(ang), dtype=F32)


def _mlp_kernel(x_ref, delta_ref, g_ref, w1_ref, w2_ref, *rest, final_norm):
    if final_norm:
        gf_ref, o_ref = rest
    else:
        (o_ref,) = rest
    d_ff = w1_ref.shape[1]
    rows = x_ref.shape[0]
    part = rows // ROW_PARTS
    hs = [x_ref[r:r + part, :] + delta_ref[r:r + part, :] for r in range(0, rows, part)]
    hns = [_rmsnorm_bf16(h, g_ref[...]) for h in hs]
    for idx, r in enumerate(range(0, rows, part)):
        hn = hns[idx]
        acc = hs[idx]
        for c in range(0, d_ff, FF_BLOCK):
            u = jnp.maximum(_dot(hn, w1_ref[:, c:c + FF_BLOCK]), 0.0)
            acc = acc + _dot((u * u).astype(BF16), w2_ref[c:c + FF_BLOCK, :])
        if final_norm:
            ms = jnp.mean(acc * acc, axis=-1, keepdims=True)
            acc = acc * lax.rsqrt(ms + EPS) * gf_ref[...]
        o_ref[r:r + part, :] = acc


def _mlp(x2, delta, g, w1_bf16, w2_bf16, final_g=None):
    tokens, d = x2.shape
    d_ff = w1_bf16.shape[1]
    tm = ROW_BLOCK
    row = lambda i: (i, 0)
    in_specs = [pl.BlockSpec((tm, d), row), pl.BlockSpec((tm, d), row),
                _resident((1, d)), _resident((d, d_ff)), _resident((d_ff, d))]
    args = [x2, delta, g, w1_bf16, w2_bf16]
    if final_g is not None:
        in_specs.append(_resident((1, d)))
        args.append(final_g)
    return pl.pallas_call(
        functools.partial(_mlp_kernel, final_norm=final_g is not None),
        grid=(tokens // tm,),
        in_specs=in_specs,
        out_specs=pl.BlockSpec((tm, d), row),
        out_shape=jax.ShapeDtypeStruct((tokens, d), F32),
        compiler_params=_params("parallel"),
        name="mlp_final" if final_g is not None else "mlp",
    )(*args)


def _att_proj_kernel(x_ref, g_ref, wqt_ref, wk_ref, wvt_ref, qt_ref, k_ref, vt_ref, *, q_scale):
    d = x_ref.shape[1]
    blk = qt_ref.shape[2]
    step = 512
    rows = x_ref.shape[0]
    part = rows // ROW_PARTS
    per = part // blk
    hns = [_rmsnorm_bf16(x_ref[r:r + part, :], g_ref[...]) for r in range(0, rows, part)]
    for idx, r in enumerate(range(0, rows, part)):
        hn = hns[idx]

        def store_feature_major(out_ref, c, t):
            for s in range(per):
                out_ref[idx * per + s, c:c + step, :] = t[:, s * blk:(s + 1) * blk].astype(BF16)

        for c in range(0, d, step):
            store_feature_major(qt_ref, c, _dot_nt(wqt_ref[c:c + step, :], hn) * q_scale)
        for c in range(0, d, step):
            k_ref[r:r + part, c:c + step] = _dot(hn, wk_ref[:, c:c + step]).astype(BF16)
        for c in range(0, d, step):
            store_feature_major(vt_ref, c, _dot_nt(wvt_ref[c:c + step, :], hn))


def _att_proj(x2, g, wqt, wk, wvt):
    tokens, d = x2.shape
    tm = ROW_BLOCK
    blk = SEQ_BLOCK
    row = lambda i: (i, 0)
    slab = lambda i: (i, 0, 0)
    return pl.pallas_call(
        functools.partial(_att_proj_kernel, q_scale=(d // ATT_HEADS) ** -0.5),
        grid=(tokens // tm,),
        in_specs=[pl.BlockSpec((tm, d), row), _resident((1, d)),
                  _resident((d, d)), _resident((d, d)), _resident((d, d))],
        out_specs=[pl.BlockSpec((tm // blk, d, blk), slab), pl.BlockSpec((tm, d), row),
                   pl.BlockSpec((tm // blk, d, blk), slab)],
        out_shape=[jax.ShapeDtypeStruct((tokens // blk, d, blk), BF16),
                   jax.ShapeDtypeStruct((tokens, d), BF16),
                   jax.ShapeDtypeStruct((tokens // blk, d, blk), BF16)],
        compiler_params=_params("parallel"),
        name="att_proj",
    )(x2, g, wqt, wk, wvt)


def _band_live_pieces(blk, sub):
    m = np.arange(3 * blk)[:, None]
    n = np.arange(blk)[None, :]
    key_chunk = m // CHUNK - PAST_CHUNKS
    query_chunk = n // CHUNK
    valid = (key_chunk <= query_chunk) & (key_chunk >= query_chunk - PAST_CHUNKS)
    pieces = valid.reshape(3, blk // sub, sub, blk // sub, sub).any(axis=(2, 4))
    return pieces.tolist()


def _att_core_kernel(qt_ref, k_ref, vt_ref, rel_ref, wout_ref, o_ref, ot_ref, bias_ref,
                     khist_ref, vhist_ref, *, heads, dh):
    b = pl.program_id(0)
    j = pl.program_id(1)
    blocks_per_step, _, blk = qt_ref.shape
    n_tiles = 3

    @pl.when((b == 0) & (j == 0))
    def _build_bias():
        r = lax.broadcasted_iota(jnp.int32, (blk, blk), 0)
        c = lax.broadcasted_iota(jnp.int32, (blk, blk), 1)
        shift = CHUNK.bit_length() - 1
        query_chunk = lax.shift_right_logical(c, shift)
        chunks_per_blk = blk // CHUNK

        def per_head(h, carry):
            for t in range(n_tiles):
                key_chunk = lax.shift_right_logical(r, shift) + (t * chunks_per_blk - PAST_CHUNKS)
                valid = (key_chunk <= query_chunk) & (key_chunk >= query_chunk - PAST_CHUNKS)
                row = jnp.broadcast_to(rel_ref[h, t:t + 1, :], (blk, 2 * blk))
                tile = pltpu.roll(row, blk, 1, stride=1, stride_axis=0)[:, :blk]
                bias_ref[h, t * blk:(t + 1) * blk, :] = jnp.where(valid, tile, -jnp.inf)
            return carry

        lax.fori_loop(0, heads, per_head, 0)

    @pl.when(j == 0)
    def _reset_history():
        khist_ref[...] = jnp.zeros_like(khist_ref)
        vhist_ref[...] = jnp.zeros_like(vhist_ref)

    zeros = jnp.zeros((dh, blk), BF16)
    ones = jnp.ones((16, blk), BF16)
    sub = V7X_LANES
    n_sub = blk // sub
    live = _band_live_pieces(blk, sub)

    def block(s, carry):
        rows = pl.ds(pl.multiple_of(s * blk, blk), blk)
        n = j * blocks_per_step + s
        cur = lax.rem(n, 3)
        older = lax.rem(n + 1, 3)
        newer = lax.rem(n + 2, 3)
        exists = (n >= 2, n >= 1, None)

        def k_tile(t, lo):
            if t == 2:
                return k_ref[rows, lo:lo + 2 * dh]
            return khist_ref[older if t == 0 else newer, :, lo:lo + 2 * dh]

        def v_tile(t, h):
            if t == 2:
                return vt_ref[s, h * dh:(h + 1) * dh, :]
            return vhist_ref[older if t == 0 else newer, h * dh:(h + 1) * dh, :]

        def pair_scores(hp):
            lo = hp * 2 * dh
            q_even = jnp.concatenate([qt_ref[s, lo:lo + dh, :], zeros], axis=0)
            q_odd = jnp.concatenate([zeros, qt_ref[s, lo + dh:lo + 2 * dh, :]], axis=0)
            q_pair = jnp.concatenate([q_even, q_odd], axis=1)
            k_all = jnp.concatenate([k_tile(t, lo) for t in range(n_tiles)], axis=0)
            s_all = _dot(k_all, q_pair)
            return [s_all[t * blk:(t + 1) * blk] for t in range(n_tiles)]

        def head_output(h, s_pair):
            par = h % 2
            p_pieces = [[[None] * n_sub for _ in range(n_sub)] for _ in range(n_tiles)]
            for ci in range(n_sub):
                cols = slice(ci * sub, (ci + 1) * sub)
                sc = {}
                m = None
                for t in range(n_tiles):
                    for ri in range(n_sub):
                        if not live[t][ri][ci]:
                            continue
                        prow = slice(ri * sub, (ri + 1) * sub)
                        piece = (s_pair[t][prow, par * blk + ci * sub:par * blk + (ci + 1) * sub]
                                 + bias_ref[h, t * blk + ri * sub:t * blk + (ri + 1) * sub, cols])
                        sc[t, ri] = piece
                        m_piece = jnp.max(piece, axis=0, keepdims=True)
                        if exists[t] is not None:
                            m_piece = jnp.where(exists[t], m_piece, -jnp.inf)
                        m = m_piece if m is None else jnp.maximum(m, m_piece)
                for t in range(n_tiles):
                    off = m if exists[t] is None else jnp.where(exists[t], m, jnp.inf)
                    for ri in range(n_sub):
                        if live[t][ri][ci]:
                            p_pieces[t][ri][ci] = jnp.exp((sc[t, ri] - off).astype(BF16))
                        else:
                            p_pieces[t][ri][ci] = jnp.zeros((sub, sub), BF16)
            p_all = jnp.concatenate([jnp.concatenate(prow, axis=1)
                                     for t in range(n_tiles) for prow in p_pieces[t]], axis=0)
            lhs_all = jnp.concatenate(
                [jnp.concatenate([v_tile(t, h), ones], axis=0) for t in range(n_tiles)], axis=1)
            acc = _dot(lhs_all, p_all)
            ot_ref[h * dh:(h + 1) * dh, :] = (acc[:dh] * (1.0 / acc[dh:dh + 1])).astype(BF16)

        n_pairs = heads // 2
        ahead = 2
        queue = [pair_scores(hp) for hp in range(ahead)]
        for hp in range(n_pairs):
            s_cur = queue.pop(0)
            head_output(2 * hp, s_cur)
            if hp + ahead < n_pairs:
                queue.append(pair_scores(hp + ahead))
            head_output(2 * hp + 1, s_cur)
        o_ref[rows, :] = _dot_tn(ot_ref[...], wout_ref[...])
        khist_ref[cur] = k_ref[rows, :]
        vhist_ref[cur] = vt_ref[s]
        return carry

    lax.fori_loop(0, blocks_per_step, block, 0)


def _att_core(qt, k, vt, rel_rows, wout_bf16, *, batch, seq):
    tokens, d = k.shape
    blk = SEQ_BLOCK
    step = SEQ_STEP
    nstep = seq // step
    dh = d // ATT_HEADS
    row = lambda b, j: (b * nstep + j, 0)
    slab = lambda b, j: (b * nstep + j, 0, 0)
    return pl.pallas_call(
        functools.partial(_att_core_kernel, heads=ATT_HEADS, dh=dh),
        grid=(batch, nstep),
        in_specs=[
            pl.BlockSpec((step // blk, d, blk), slab),
            pl.BlockSpec((step, d), row),
            pl.BlockSpec((step // blk, d, blk), slab),
            _resident(rel_rows.shape),
            _resident((d, d)),
        ],
        out_specs=pl.BlockSpec((step, d), row),
        out_shape=jax.ShapeDtypeStruct((tokens, d), F32),
        scratch_shapes=[pltpu.VMEM((d, blk), BF16),
                        pltpu.VMEM((ATT_HEADS, 3 * blk, blk), F32),
                        pltpu.VMEM((3, blk, d), BF16),
                        pltpu.VMEM((3, d, blk), BF16)],
        compiler_params=_params("arbitrary", "arbitrary"),
        name="att_core",
    )(qt, k, vt, rel_rows, wout_bf16)


def _attention_rel_rows(rel_bias):
    blk = SEQ_BLOCK
    assert PAST_CHUNKS * CHUNK == 2 * blk
    a = np.arange(2 * blk)[None, :]
    t = np.arange(3)[:, None]
    rel = blk * (t - 1) - a
    idx = np.clip(np.maximum(rel, -MAX_REL) + MAX_REL, 0, rel_bias.shape[1] - 1)
    return rel_bias.astype(F32)[:, idx]


def kernel(x, mix_norm_g, ret_w_in, ret_gn_g, ret_w_out, att_w_in, att_rel_bias, att_w_out,
           mlp_norm_g, mlp_w1, mlp_w2, final_norm_g):
    batch, seq, d = x.shape
    depth = mix_norm_g.shape[0]
    assert seq % ROW_BLOCK == 0 and seq % SEQ_STEP == 0
    assert ROW_BLOCK % (ROW_PARTS * SEQ_BLOCK) == 0 and SEQ_STEP % SEQ_BLOCK == 0 and SEQ_BLOCK % CHUNK == 0
    h = x.reshape(batch * seq, d)
    row_vec = lambda v: v.reshape(1, -1).astype(F32)

    for i in range(depth):
        jdx = i // 2
        if i % 2 == 0:
            w_in = ret_w_in[jdx].astype(BF16)
            dk = d // RET_HEADS
            cos, sin = _rope_tables(seq, dk)
            mask, qdec, kdec, block_decay = _retention_tables(dk)
            q, k, v, gate = _ret_proj(h, row_vec(mix_norm_g[i]), w_in, cos, sin, qdec, kdec,
                                      row_vec(ret_gn_g[jdx]), seq=seq)
            delta = _ret_core(q, k, v, gate, mask, block_decay, ret_w_out[jdx].astype(BF16),
                              batch=batch, seq=seq)
        else:
            w_in = att_w_in[jdx]
            wqt = w_in[:, :d].T.astype(BF16)
            wk = w_in[:, d:2 * d].astype(BF16)
            wvt = w_in[:, 2 * d:].T.astype(BF16)
            qt, k, vt = _att_proj(h, row_vec(mix_norm_g[i]), wqt, wk, wvt)
            delta = _att_core(qt, k, vt, _attention_rel_rows(att_rel_bias[jdx]),
                              att_w_out[jdx].astype(BF16), batch=batch, seq=seq)
        last = i == depth - 1
        h = _mlp(h, delta, row_vec(mlp_norm_g[i]), mlp_w1[i].astype(BF16), mlp_w2[i].astype(BF16),
                 row_vec(final_norm_g) if last else None)
    if depth == 0:
        raise NotImplementedError("depth 0 has no Pallas stage")
    return h.reshape(batch, seq, d)
```

```python
import functools

import numpy as np
import jax
import jax.numpy as jnp
from jax import lax
from jax.experimental import pallas as pl
from jax.experimental.pallas import tpu as pltpu

CHUNK = 64
RET_HEADS = 4
ATT_HEADS = 16
PAST_CHUNKS = 8
MAX_REL = 256
ROPE_BASE = 10000.0
EPS = 1e-6

V7X_LANES = 128
V7X_VMEM_LIMIT_BYTES = 60000 * 1024

ROW_BLOCK = 1024
ROW_PARTS = 4
SEQ_BLOCK = 256
SEQ_STEP = 512
FF_BLOCK = 1024

BF16 = jnp.bfloat16
F32 = jnp.float32


def _params(*semantics):
    return pltpu.CompilerParams(dimension_semantics=semantics,
                                vmem_limit_bytes=V7X_VMEM_LIMIT_BYTES)


def _resident(shape):
    zeros = (0,) * len(shape)
    return pl.BlockSpec(shape, lambda *_: zeros, pipeline_mode=pl.Buffered(1))


def _rmsnorm_bf16(x, g):
    ms = jnp.mean(x * x, axis=-1, keepdims=True)
    return (x * lax.rsqrt(ms + EPS) * g).astype(BF16)


def _dot(a, b):
    return jnp.dot(a, b, preferred_element_type=F32)


def _dot_nt(a, b):
    return lax.dot_general(a, b, (((1,), (1,)), ((), ())), preferred_element_type=F32)


def _dot_tn(a, b):
    return lax.dot_general(a, b, (((0,), (0,)), ((), ())), preferred_element_type=F32)


def _ret_proj_kernel(x_ref, g_ref, w_ref, cos_ref, sin_ref, qdec_ref, kdec_ref, gn_ref,
                     q_ref, k_ref, v_ref, gate_ref, *, heads, dk):
    hn = _rmsnorm_bf16(x_ref[...], g_ref[...])
    cos = cos_ref[...]
    sin = sin_ref[...]
    half = dk // 2
    qk_width = heads * dk

    def rope_store(out_ref, col0, dec_ref):
        for h in range(heads):
            t = _dot(hn, w_ref[:, col0 + h * dk: col0 + (h + 1) * dk])
            t1 = t[:, :half]
            t2 = t[:, half:]
            dec = dec_ref[h]
            out_ref[:, h * dk: h * dk + half] = ((t1 * cos - t2 * sin) * dec).astype(BF16)
            out_ref[:, h * dk + half: (h + 1) * dk] = ((t1 * sin + t2 * cos) * dec).astype(BF16)

    vw = v_ref.shape[1]
    step = 512
    for c in range(0, vw, step):
        gate = _dot(hn, w_ref[:, 2 * qk_width + vw + c: 2 * qk_width + vw + c + step])
        silu2 = gate * (1.0 + jnp.tanh(0.5 * gate))
        gate_ref[:, c:c + step] = (silu2 * (0.5 * gn_ref[:, c:c + step])).astype(BF16)
    rope_store(q_ref, 0, qdec_ref)
    rope_store(k_ref, qk_width, kdec_ref)
    for c in range(0, vw, step):
        v_ref[:, c:c + step] = _dot(hn, w_ref[:, 2 * qk_width + c: 2 * qk_width + c + step]).astype(BF16)


def _ret_proj(x2, g, w_bf16, cos, sin, qdec, kdec, gn_g, *, seq):
    tokens, d = x2.shape
    n_all = w_bf16.shape[1]
    qk = d
    vw = (n_all - 2 * qk) // 2
    dk = qk // RET_HEADS
    tm = ROW_BLOCK
    blocks_per_seq = seq // tm
    row = lambda i: (i, 0)
    pos = lambda i: (i % blocks_per_seq, 0)
    return pl.pallas_call(
        functools.partial(_ret_proj_kernel, heads=RET_HEADS, dk=dk),
        grid=(tokens // tm,),
        in_specs=[
            pl.BlockSpec((tm, d), row),
            _resident((1, d)),
            _resident((d, n_all)),
            pl.BlockSpec((tm, dk // 2), pos),
            pl.BlockSpec((tm, dk // 2), pos),
            _resident(qdec.shape),
            _resident(kdec.shape),
            _resident((1, vw)),
        ],
        out_specs=[
            pl.BlockSpec((tm, qk), row),
            pl.BlockSpec((tm, qk), row),
            pl.BlockSpec((tm, vw), row),
            pl.BlockSpec((tm, vw), row),
        ],
        out_shape=[
            jax.ShapeDtypeStruct((tokens, qk), BF16),
            jax.ShapeDtypeStruct((tokens, qk), BF16),
            jax.ShapeDtypeStruct((tokens, vw), BF16),
            jax.ShapeDtypeStruct((tokens, vw), BF16),
        ],
        compiler_params=_params("parallel"),
        name="ret_proj",
    )(x2, g, w_bf16, cos, sin, qdec, kdec, gn_g)


def _ret_core_kernel(q_ref, k_ref, v_ref, gate_ref, mask_ref, wout_ref, o_ref, state_ref,
                     *, heads, dk, dv, blk, block_decay):
    @pl.when(pl.program_id(1) == 0)
    def _():
        state_ref[...] = jnp.zeros_like(state_ref)

    def block(s, carry):
        rows = pl.ds(pl.multiple_of(s * blk, blk), blk)

        def retention(h):
            qh = q_ref[rows, h * dk:(h + 1) * dk]
            kh = k_ref[rows, h * dk:(h + 1) * dk]
            vh = v_ref[rows, h * dv:(h + 1) * dv]
            state = state_ref[h]
            scores = _dot_nt(qh, kh) * mask_ref[h]
            o = _dot(scores.astype(BF16), vh) + _dot(qh, state.astype(BF16))
            state_ref[h] = state * block_decay[h] + _dot_tn(kh, vh)
            return o

        def normed_gated(h, o):
            mu = jnp.mean(o, axis=-1, keepdims=True)
            cen = o - mu
            var = jnp.mean(cen * cen, axis=-1, keepdims=True)
            y = cen * lax.rsqrt(var + EPS) * gate_ref[rows, h * dv:(h + 1) * dv].astype(F32)
            return y.astype(BF16)

        acc = None
        o_next = retention(0)
        for h in range(heads):
            o_cur = o_next
            if h + 1 < heads:
                o_next = retention(h + 1)
            part = _dot(normed_gated(h, o_cur), wout_ref[h * dv:(h + 1) * dv, :])
            acc = part if acc is None else acc + part
        o_ref[rows, :] = acc
        return carry

    lax.fori_loop(0, q_ref.shape[0] // blk, block, 0)


def _ret_core(q, k, v, gate, mask, block_decay, wout_bf16, *, batch, seq):
    tokens, qk = q.shape
    d = wout_bf16.shape[1]
    vw = v.shape[1]
    dk = qk // RET_HEADS
    dv = vw // RET_HEADS
    step = SEQ_STEP
    nstep = seq // step
    row = lambda b, j: (b * nstep + j, 0)
    return pl.pallas_call(
        functools.partial(_ret_core_kernel, heads=RET_HEADS, dk=dk, dv=dv, blk=SEQ_BLOCK,
                          block_decay=block_decay),
        grid=(batch, nstep),
        in_specs=[
            pl.BlockSpec((step, qk), row),
            pl.BlockSpec((step, qk), row),
            pl.BlockSpec((step, vw), row),
            pl.BlockSpec((step, vw), row),
            _resident(mask.shape),
            _resident((vw, d)),
        ],
        out_specs=pl.BlockSpec((step, d), row),
        out_shape=jax.ShapeDtypeStruct((tokens, d), F32),
        scratch_shapes=[pltpu.VMEM((RET_HEADS, dk, dv), F32)],
        compiler_params=_params("parallel", "arbitrary"),
        name="ret_core",
    )(q, k, v, gate, mask, wout_bf16)


def _retention_tables(dk):
    blk = SEQ_BLOCK
    log_gamma = np.log1p(-np.exp2(-5.0 - np.arange(RET_HEADS, dtype=np.float64)))
    n = np.arange(blk)
    diff = n[:, None] - n[None, :]
    same = (n[:, None] // CHUNK) == (n[None, :] // CHUNK)
    earlier = (n[None, :] // CHUNK) < (n[:, None] // CHUNK)
    lg = log_gamma[:, None, None]
    log_mask = np.where(same[None], lg * np.abs(diff)[None], lg * diff[None])
    log_qk = lg * ((n + 1.0)[:, None] + (blk - 1.0 - n)[None, :])[None]
    mask = np.where((same | earlier)[None], np.exp(log_mask - log_qk), 0.0)
    qdec = np.exp(log_gamma[:, None] * (n + 1.0)[None, :])
    kdec = np.exp(log_gamma[:, None] * (blk - 1.0 - n)[None, :]) * dk ** -0.5
    block_decay = tuple(float(c) for c in np.exp(log_gamma * blk))
    reps = ROW_BLOCK // blk
    lanes = dk // 2
    per_row = lambda a: np.broadcast_to(np.tile(a, (1, reps))[:, :, None], (RET_HEADS, ROW_BLOCK, lanes))
    as_f32 = lambda a: jnp.asarray(np.ascontiguousarray(a), dtype=F32)
    return as_f32(mask), as_f32(per_row(qdec)), as_f32(per_row(kdec)), block_decay


def _rope_tables(seq, dk):
    half = dk // 2
    inv = np.exp(-np.log(ROPE_BASE) * np.arange(half, dtype=np.float64) / half)
    ang = np.arange(seq, dtype=np.float64)[:, None] * inv[None, :]
    return jnp.asarray(np.cos(ang), dtype=F32), jnp.asarray(np.sin(ang), dtype=F32)


def _mlp_kernel(x_ref, delta_ref, g_ref, w1_ref, w2_ref, *rest, final_norm):
    if final_norm:
        gf_ref, o_ref = rest
    else:
        (o_ref,) = rest
    d_ff = w1_ref.shape[1]
    rows = x_ref.shape[0]
    part = rows // ROW_PARTS
    hs = [x_ref[r:r + part, :] + delta_ref[r:r + part, :] for r in range(0, rows, part)]
    hns = [_rmsnorm_bf16(h, g_ref[...]) for h in hs]
    for idx, r in enumerate(range(0, rows, part)):
        hn = hns[idx]
        acc = hs[idx]
        for c in range(0, d_ff, FF_BLOCK):
            u = jnp.maximum(_dot(hn, w1_ref[:, c:c + FF_BLOCK]), 0.0)
            acc = acc + _dot((u * u).astype(BF16), w2_ref[c:c + FF_BLOCK, :])
        if final_norm:
            ms = jnp.mean(acc * acc, axis=-1, keepdims=True)
            acc = acc * lax.rsqrt(ms + EPS) * gf_ref[...]
        o_ref[r:r + part, :] = acc


def _mlp(x2, delta, g, w1_bf16, w2_bf16, final_g=None):
    tokens, d = x2.shape
    d_ff = w1_bf16.shape[1]
    tm = ROW_BLOCK
    row = lambda i: (i, 0)
    in_specs = [pl.BlockSpec((tm, d), row), pl.BlockSpec((tm, d), row),
                _resident((1, d)), _resident((d, d_ff)), _resident((d_ff, d))]
    args = [x2, delta, g, w1_bf16, w2_bf16]
    if final_g is not None:
        in_specs.append(_resident((1, d)))
        args.append(final_g)
    return pl.pallas_call(
        functools.partial(_mlp_kernel, final_norm=final_g is not None),
        grid=(tokens // tm,),
        in_specs=in_specs,
        out_specs=pl.BlockSpec((tm, d), row),
        out_shape=jax.ShapeDtypeStruct((tokens, d), F32),
        compiler_params=_params("parallel"),
        name="mlp_final" if final_g is not None else "mlp",
    )(*args)


def _att_proj_kernel(x_ref, g_ref, wqt_ref, wk_ref, wvt_ref, qt_ref, k_ref, vt_ref, *, q_scale):
    d = x_ref.shape[1]
    blk = qt_ref.shape[2]
    step = 512
    rows = x_ref.shape[0]
    part = rows // ROW_PARTS
    per = part // blk
    hns = [_rmsnorm_bf16(x_ref[r:r + part, :], g_ref[...]) for r in range(0, rows, part)]
    for idx, r in enumerate(range(0, rows, part)):
        hn = hns[idx]

        def store_feature_major(out_ref, c, t):
            for s in range(per):
                out_ref[idx * per + s, c:c + step, :] = t[:, s * blk:(s + 1) * blk].astype(BF16)

        for c in range(0, d, step):
            store_feature_major(qt_ref, c, _dot_nt(wqt_ref[c:c + step, :], hn) * q_scale)
        for c in range(0, d, step):
            k_ref[r:r + part, c:c + step] = _dot(hn, wk_ref[:, c:c + step]).astype(BF16)
        for c in range(0, d, step):
            store_feature_major(vt_ref, c, _dot_nt(wvt_ref[c:c + step, :], hn))


def _att_proj(x2, g, wqt, wk, wvt):
    tokens, d = x2.shape
    tm = ROW_BLOCK
    blk = SEQ_BLOCK
    row = lambda i: (i, 0)
    slab = lambda i: (i, 0, 0)
    return pl.pallas_call(
        functools.partial(_att_proj_kernel, q_scale=(d // ATT_HEADS) ** -0.5),
        grid=(tokens // tm,),
        in_specs=[pl.BlockSpec((tm, d), row), _resident((1, d)),
                  _resident((d, d)), _resident((d, d)), _resident((d, d))],
        out_specs=[pl.BlockSpec((tm // blk, d, blk), slab), pl.BlockSpec((tm, d), row),
                   pl.BlockSpec((tm // blk, d, blk), slab)],
        out_shape=[jax.ShapeDtypeStruct((tokens // blk, d, blk), BF16),
                   jax.ShapeDtypeStruct((tokens, d), BF16),
                   jax.ShapeDtypeStruct((tokens // blk, d, blk), BF16)],
        compiler_params=_params("parallel"),
        name="att_proj",
    )(x2, g, wqt, wk, wvt)


def _band_live_pieces(blk, sub):
    m = np.arange(3 * blk)[:, None]
    n = np.arange(blk)[None, :]
    key_chunk = m // CHUNK - PAST_CHUNKS
    query_chunk = n // CHUNK
    valid = (key_chunk <= query_chunk) & (key_chunk >= query_chunk - PAST_CHUNKS)
    pieces = valid.reshape(3, blk // sub, sub, blk // sub, sub).any(axis=(2, 4))
    return pieces.tolist()


def _att_core_kernel(qt_ref, k_ref, vt_ref, rel_ref, wout_ref, o_ref, ot_ref, bias_ref,
                     khist_ref, vhist_ref, *, heads, dh):
    b = pl.program_id(0)
    j = pl.program_id(1)
    blocks_per_step, _, blk = qt_ref.shape
    n_tiles = 3

    @pl.when((b == 0) & (j == 0))
    def _build_bias():
        r = lax.broadcasted_iota(jnp.int32, (blk, blk), 0)
        c = lax.broadcasted_iota(jnp.int32, (blk, blk), 1)
        shift = CHUNK.bit_length() - 1
        query_chunk = lax.shift_right_logical(c, shift)
        chunks_per_blk = blk // CHUNK

        def per_head(h, carry):
            for t in range(n_tiles):
                key_chunk = lax.shift_right_logical(r, shift) + (t * chunks_per_blk - PAST_CHUNKS)
                valid = (key_chunk <= query_chunk) & (key_chunk >= query_chunk - PAST_CHUNKS)
                row = jnp.broadcast_to(rel_ref[h, t:t + 1, :], (blk, 2 * blk))
                tile = pltpu.roll(row, blk, 1, stride=1, stride_axis=0)[:, :blk]
                bias_ref[h, t * blk:(t + 1) * blk, :] = jnp.where(valid, tile, -jnp.inf)
            return carry

        lax.fori_loop(0, heads, per_head, 0)

    @pl.when(j == 0)
    def _reset_history():
        khist_ref[...] = jnp.zeros_like(khist_ref)
        vhist_ref[...] = jnp.zeros_like(vhist_ref)

    zeros = jnp.zeros((dh, blk), BF16)
    ones = jnp.ones((16, blk), BF16)
    sub = V7X_LANES
    n_sub = blk // sub
    live = _band_live_pieces(blk, sub)

    def block(s, carry):
        rows = pl.ds(pl.multiple_of(s * blk, blk), blk)
        n = j * blocks_per_step + s
        cur = lax.rem(n, 3)
        older = lax.rem(n + 1, 3)
        newer = lax.rem(n + 2, 3)
        exists = (n >= 2, n >= 1, None)

        def k_tile(t, lo):
            if t == 2:
                return k_ref[rows, lo:lo + 2 * dh]
            return khist_ref[older if t == 0 else newer, :, lo:lo + 2 * dh]

        def v_tile(t, h):
            if t == 2:
                return vt_ref[s, h * dh:(h + 1) * dh, :]
            return vhist_ref[older if t == 0 else newer, h * dh:(h + 1) * dh, :]

        def pair_scores(hp):
            lo = hp * 2 * dh
            q_even = jnp.concatenate([qt_ref[s, lo:lo + dh, :], zeros], axis=0)
            q_odd = jnp.concatenate([zeros, qt_ref[s, lo + dh:lo + 2 * dh, :]], axis=0)
            q_pair = jnp.concatenate([q_even, q_odd], axis=1)
            k_all = jnp.concatenate([k_tile(t, lo) for t in range(n_tiles)], axis=0)
            s_all = _dot(k_all, q_pair)
            return [s_all[t * blk:(t + 1) * blk] for t in range(n_tiles)]

        def head_output(h, s_pair):
            par = h % 2
            p_pieces = [[[None] * n_sub for _ in range(n_sub)] for _ in range(n_tiles)]
            for ci in range(n_sub):
                cols = slice(ci * sub, (ci + 1) * sub)
                sc = {}
                m = None
                for t in range(n_tiles):
                    for ri in range(n_sub):
                        if not live[t][ri][ci]:
                            continue
                        prow = slice(ri * sub, (ri + 1) * sub)
                        piece = (s_pair[t][prow, par * blk + ci * sub:par * blk + (ci + 1) * sub]
                                 + bias_ref[h, t * blk + ri * sub:t * blk + (ri + 1) * sub, cols])
                        sc[t, ri] = piece
                        m_piece = jnp.max(piece, axis=0, keepdims=True)
                        if exists[t] is not None:
                            m_piece = jnp.where(exists[t], m_piece, -jnp.inf)
                        m = m_piece if m is None else jnp.maximum(m, m_piece)
                for t in range(n_tiles):
                    off = m if exists[t] is None else jnp.where(exists[t], m, jnp.inf)
                    for ri in range(n_sub):
                        if live[t][ri][ci]:
                            p_pieces[t][ri][ci] = jnp.exp((sc[t, ri] - off).astype(BF16))
                        else:
                            p_pieces[t][ri][ci] = jnp.zeros((sub, sub), BF16)
            p_all = jnp.concatenate([jnp.concatenate(prow, axis=1)
                                     for t in range(n_tiles) for prow in p_pieces[t]], axis=0)
            lhs_all = jnp.concatenate(
                [jnp.concatenate([v_tile(t, h), ones], axis=0) for t in range(n_tiles)], axis=1)
            acc = _dot(lhs_all, p_all)
            ot_ref[h * dh:(h + 1) * dh, :] = (acc[:dh] * (1.0 / acc[dh:dh + 1])).astype(BF16)

        n_pairs = heads // 2
        ahead = 2
        queue = [pair_scores(hp) for hp in range(ahead)]
        for hp in range(n_pairs):
            s_cur = queue.pop(0)
            head_output(2 * hp, s_cur)
            if hp + ahead < n_pairs:
                queue.append(pair_scores(hp + ahead))
            head_output(2 * hp + 1, s_cur)
        o_ref[rows, :] = _dot_tn(ot_ref[...], wout_ref[...])
        khist_ref[cur] = k_ref[rows, :]
        vhist_ref[cur] = vt_ref[s]
        return carry

    lax.fori_loop(0, blocks_per_step, block, 0)


def _att_core(qt, k, vt, rel_rows, wout_bf16, *, batch, seq):
    tokens, d = k.shape
    blk = SEQ_BLOCK
    step = SEQ_STEP
    nstep = seq // step
    dh = d // ATT_HEADS
    row = lambda b, j: (b * nstep + j, 0)
    slab = lambda b, j: (b * nstep + j, 0, 0)
    return pl.pallas_call(
        functools.partial(_att_core_kernel, heads=ATT_HEADS, dh=dh),
        grid=(batch, nstep),
        in_specs=[
            pl.BlockSpec((step // blk, d, blk), slab),
            pl.BlockSpec((step, d), row),
            pl.BlockSpec((step // blk, d, blk), slab),
            _resident(rel_rows.shape),
            _resident((d, d)),
        ],
        out_specs=pl.BlockSpec((step, d), row),
        out_shape=jax.ShapeDtypeStruct((tokens, d), F32),
        scratch_shapes=[pltpu.VMEM((d, blk), BF16),
                        pltpu.VMEM((ATT_HEADS, 3 * blk, blk), F32),
                        pltpu.VMEM((3, blk, d), BF16),
                        pltpu.VMEM((3, d, blk), BF16)],
        compiler_params=_params("arbitrary", "arbitrary"),
        name="att_core",
    )(qt, k, vt, rel_rows, wout_bf16)


def _attention_rel_rows(rel_bias):
    blk = SEQ_BLOCK
    assert PAST_CHUNKS * CHUNK == 2 * blk
    a = np.arange(2 * blk)[None, :]
    t = np.arange(3)[:, None]
    rel = blk * (t - 1) - a
    idx = np.clip(np.maximum(rel, -MAX_REL) + MAX_REL, 0, rel_bias.shape[1] - 1)
    return rel_bias.astype(F32)[:, idx]


def kernel(x, mix_norm_g, ret_w_in, ret_gn_g, ret_w_out, att_w_in, att_rel_bias, att_w_out,
           mlp_norm_g, mlp_w1, mlp_w2, final_norm_g):
    batch, seq, d = x.shape
    depth = mix_norm_g.shape[0]
    assert seq % ROW_BLOCK == 0 and seq % SEQ_STEP == 0
    assert ROW_BLOCK % (ROW_PARTS * SEQ_BLOCK) == 0 and SEQ_STEP % SEQ_BLOCK == 0 and SEQ_BLOCK % CHUNK == 0
    h = x.reshape(batch * seq, d)
    row_vec = lambda v: v.reshape(1, -1).astype(F32)

    for i in range(depth):
        jdx = i // 2
        if i % 2 == 0:
            w_in = ret_w_in[jdx].astype(BF16)
            dk = d // RET_HEADS
            cos, sin = _rope_tables(seq, dk)
            mask, qdec, kdec, block_decay = _retention_tables(dk)
            q, k, v, gate = _ret_proj(h, row_vec(mix_norm_g[i]), w_in, cos, sin, qdec, kdec,
                                      row_vec(ret_gn_g[jdx]), seq=seq)
            delta = _ret_core(q, k, v, gate, mask, block_decay, ret_w_out[jdx].astype(BF16),
                              batch=batch, seq=seq)
        else:
            w_in = att_w_in[jdx]
            wqt = w_in[:, :d].T.astype(BF16)
            wk = w_in[:, d:2 * d].astype(BF16)
            wvt = w_in[:, 2 * d:].T.astype(BF16)
            qt, k, vt = _att_proj(h, row_vec(mix_norm_g[i]), wqt, wk, wvt)
            delta = _att_core(qt, k, vt, _attention_rel_rows(att_rel_bias[jdx]),
                              att_w_out[jdx].astype(BF16), batch=batch, seq=seq)
        last = i == depth - 1
        h = _mlp(h, delta, row_vec(mlp_norm_g[i]), mlp_w1[i].astype(BF16), mlp_w2[i].astype(BF16),
                 row_vec(final_norm_g) if last else None)
    if depth == 0:
        raise NotImplementedError("depth 0 has no Pallas stage")
    return h.reshape(batch, seq, d)
```

```python
import functools

import numpy as np
import jax
import jax.numpy as jnp
from jax import lax
from jax.experimental import pallas as pl
from jax.experimental.pallas import tpu as pltpu

CHUNK = 64
RET_HEADS = 4
ATT_HEADS = 16
PAST_CHUNKS = 8
MAX_REL = 256
ROPE_BASE = 10000.0
EPS = 1e-6

V7X_LANES = 128
V7X_VMEM_LIMIT_BYTES = 60000 * 1024

ROW_BLOCK = 1024
ROW_PARTS = 4
SEQ_BLOCK = 256
SEQ_STEP = 1024
SCORE_SLOTS = 3
FF_BLOCK = 1024

BF16 = jnp.bfloat16
F32 = jnp.float32


def _params(*semantics):
    return pltpu.CompilerParams(dimension_semantics=semantics,
                                vmem_limit_bytes=V7X_VMEM_LIMIT_BYTES)


def _resident(shape):
    zeros = (0,) * len(shape)
    return pl.BlockSpec(shape, lambda *_: zeros, pipeline_mode=pl.Buffered(1))


def _rmsnorm_bf16(x, g):
    ms = jnp.mean(x * x, axis=-1, keepdims=True)
    return (x * lax.rsqrt(ms + EPS) * g).astype(BF16)


def _dot(a, b):
    return jnp.dot(a, b, preferred_element_type=F32)


def _dot_nt(a, b):
    return lax.dot_general(a, b, (((1,), (1,)), ((), ())), preferred_element_type=F32)


def _dot_tn(a, b):
    return lax.dot_general(a, b, (((0,), (0,)), ((), ())), preferred_element_type=F32)


def _ret_proj_kernel(x_ref, g_ref, w_ref, cos_ref, sin_ref, qdec_ref, kdec_ref, gn_ref,
                     q_ref, k_ref, v_ref, gate_ref, *, heads, dk):
    hn = _rmsnorm_bf16(x_ref[...], g_ref[...])
    cos = cos_ref[...]
    sin = sin_ref[...]
    half = dk // 2
    qk_width = heads * dk

    def rope_store(out_ref, col0, dec_ref):
        for h in range(heads):
            t = _dot(hn, w_ref[:, col0 + h * dk: col0 + (h + 1) * dk])
            t1 = t[:, :half]
            t2 = t[:, half:]
            dec = dec_ref[h]
            out_ref[:, h * dk: h * dk + half] = ((t1 * cos - t2 * sin) * dec).astype(BF16)
            out_ref[:, h * dk + half: (h + 1) * dk] = ((t1 * sin + t2 * cos) * dec).astype(BF16)

    vw = v_ref.shape[1]
    step = 512
    for c in range(0, vw, step):
        gate = _dot(hn, w_ref[:, 2 * qk_width + vw + c: 2 * qk_width + vw + c + step])
        silu2 = gate * (1.0 + jnp.tanh(0.5 * gate))
        gate_ref[:, c:c + step] = (silu2 * (0.5 * gn_ref[:, c:c + step])).astype(BF16)
    rope_store(q_ref, 0, qdec_ref)
    rope_store(k_ref, qk_width, kdec_ref)
    for c in range(0, vw, step):
        v_ref[:, c:c + step] = _dot(hn, w_ref[:, 2 * qk_width + c: 2 * qk_width + c + step]).astype(BF16)


def _ret_proj(x2, g, w_bf16, cos, sin, qdec, kdec, gn_g, *, seq):
    tokens, d = x2.shape
    n_all = w_bf16.shape[1]
    qk = d
    vw = (n_all - 2 * qk) // 2
    dk = qk // RET_HEADS
    tm = ROW_BLOCK
    blocks_per_seq = seq // tm
    row = lambda i: (i, 0)
    pos = lambda i: (i % blocks_per_seq, 0)
    return pl.pallas_call(
        functools.partial(_ret_proj_kernel, heads=RET_HEADS, dk=dk),
        grid=(tokens // tm,),
        in_specs=[
            pl.BlockSpec((tm, d), row),
            _resident((1, d)),
            _resident((d, n_all)),
            pl.BlockSpec((tm, dk // 2), pos),
            pl.BlockSpec((tm, dk // 2), pos),
            _resident(qdec.shape),
            _resident(kdec.shape),
            _resident((1, vw)),
        ],
        out_specs=[
            pl.BlockSpec((tm, qk), row),
            pl.BlockSpec((tm, qk), row),
            pl.BlockSpec((tm, vw), row),
            pl.BlockSpec((tm, vw), row),
        ],
        out_shape=[
            jax.ShapeDtypeStruct((tokens, qk), BF16),
            jax.ShapeDtypeStruct((tokens, qk), BF16),
            jax.ShapeDtypeStruct((tokens, vw), BF16),
            jax.ShapeDtypeStruct((tokens, vw), BF16),
        ],
        compiler_params=_params("parallel"),
        name="ret_proj",
    )(x2, g, w_bf16, cos, sin, qdec, kdec, gn_g)


def _ret_core_kernel(q_ref, k_ref, v_ref, gate_ref, mask_ref, wout_ref, o_ref, state_ref,
                     *, heads, dk, dv, blk, block_decay):
    @pl.when(pl.program_id(1) == 0)
    def _():
        state_ref[...] = jnp.zeros_like(state_ref)

    def block(s, carry):
        rows = pl.ds(pl.multiple_of(s * blk, blk), blk)

        def retention(h):
            qh = q_ref[rows, h * dk:(h + 1) * dk]
            kh = k_ref[rows, h * dk:(h + 1) * dk]
            vh = v_ref[rows, h * dv:(h + 1) * dv]
            state = state_ref[h]
            scores = _dot_nt(qh, kh) * mask_ref[h]
            o = _dot(scores.astype(BF16), vh) + _dot(qh, state.astype(BF16))
            state_ref[h] = state * block_decay[h] + _dot_tn(kh, vh)
            return o

        def normed_gated(h, o):
            mu = jnp.mean(o, axis=-1, keepdims=True)
            cen = o - mu
            var = jnp.mean(cen * cen, axis=-1, keepdims=True)
            y = cen * lax.rsqrt(var + EPS) * gate_ref[rows, h * dv:(h + 1) * dv].astype(F32)
            return y.astype(BF16)

        acc = None
        o_next = retention(0)
        for h in range(heads):
            o_cur = o_next
            if h + 1 < heads:
                o_next = retention(h + 1)
            part = _dot(normed_gated(h, o_cur), wout_ref[h * dv:(h + 1) * dv, :])
            acc = part if acc is None else acc + part
        o_ref[rows, :] = acc
        return carry

    lax.fori_loop(0, q_ref.shape[0] // blk, block, 0)


def _ret_core(q, k, v, gate, mask, block_decay, wout_bf16, *, batch, seq):
    tokens, qk = q.shape
    d = wout_bf16.shape[1]
    vw = v.shape[1]
    dk = qk // RET_HEADS
    dv = vw // RET_HEADS
    step = SEQ_STEP
    nstep = seq // step
    row = lambda b, j: (b * nstep + j, 0)
    return pl.pallas_call(
        functools.partial(_ret_core_kernel, heads=RET_HEADS, dk=dk, dv=dv, blk=SEQ_BLOCK,
                          block_decay=block_decay),
        grid=(batch, nstep),
        in_specs=[
            pl.BlockSpec((step, qk), row),
            pl.BlockSpec((step, qk), row),
            pl.BlockSpec((step, vw), row),
            pl.BlockSpec((step, vw), row),
            _resident(mask.shape),
            _resident((vw, d)),
        ],
        out_specs=pl.BlockSpec((step, d), row),
        out_shape=jax.ShapeDtypeStruct((tokens, d), F32),
        scratch_shapes=[pltpu.VMEM((RET_HEADS, dk, dv), F32)],
        compiler_params=_params("parallel", "arbitrary"),
        name="ret_core",
    )(q, k, v, gate, mask, wout_bf16)


def _retention_tables(dk):
    blk = SEQ_BLOCK
    log_gamma = np.log1p(-np.exp2(-5.0 - np.arange(RET_HEADS, dtype=np.float64)))
    n = np.arange(blk)
    diff = n[:, None] - n[None, :]
    same = (n[:, None] // CHUNK) == (n[None, :] // CHUNK)
    earlier = (n[None, :] // CHUNK) < (n[:, None] // CHUNK)
    lg = log_gamma[:, None, None]
    log_mask = np.where(same[None], lg * np.abs(diff)[None], lg * diff[None])
    log_qk = lg * ((n + 1.0)[:, None] + (blk - 1.0 - n)[None, :])[None]
    mask = np.where((same | earlier)[None], np.exp(log_mask - log_qk), 0.0)
    qdec = np.exp(log_gamma[:, None] * (n + 1.0)[None, :])
    kdec = np.exp(log_gamma[:, None] * (blk - 1.0 - n)[None, :]) * dk ** -0.5
    block_decay = tuple(float(c) for c in np.exp(log_gamma * blk))
    reps = ROW_BLOCK // blk
    lanes = dk // 2
    per_row = lambda a: np.broadcast_to(np.tile(a, (1, reps))[:, :, None], (RET_HEADS, ROW_BLOCK, lanes))
    as_f32 = lambda a: jnp.asarray(np.ascontiguousarray(a), dtype=F32)
    return as_f32(mask), as_f32(per_row(qdec)), as_f32(per_row(kdec)), block_decay


def _rope_tables(seq, dk):
    half = dk // 2
    inv = np.exp(-np.log(ROPE_BASE) * np.arange(half, dtype=np.float64) / half)
    ang = np.arange(seq, dtype=np.float64)[:, None] * inv[None, :]
    return jnp.asarray(np.cos(ang), dtype=F32), jnp.asarray(np.sin(ang), dtype=F32)


def _mlp_kernel(x_ref, delta_ref, g_ref, w1_ref, w2_ref, *rest, final_norm):
    if final_norm:
        gf_ref, o_ref = rest
    else:
        (o_ref,) = rest
    d_ff = w1_ref.shape[1]
    rows = x_ref.shape[0]
    part = rows // ROW_PARTS
    hs = [x_ref[r:r + part, :] + delta_ref[r:r + part, :] for r in range(0, rows, part)]
    hns = [_rmsnorm_bf16(h, g_ref[...]) for h in hs]
    for idx, r in enumerate(range(0, rows, part)):
        hn = hns[idx]
        acc = hs[idx]
        for c in range(0, d_ff, FF_BLOCK):
            u = jnp.maximum(_dot(hn, w1_ref[:, c:c + FF_BLOCK]), 0.0)
            acc = acc + _dot((u * u).astype(BF16), w2_ref[c:c + FF_BLOCK, :])
        if final_norm:
            ms = jnp.mean(acc * acc, axis=-1, keepdims=True)
            acc = acc * lax.rsqrt(ms + EPS) * gf_ref[...]
        o_ref[r:r + part, :] = acc


def _mlp(x2, delta, g, w1_bf16, w2_bf16, final_g=None):
    tokens, d = x2.shape
    d_ff = w1_bf16.shape[1]
    tm = ROW_BLOCK
    row = lambda i: (i, 0)
    in_specs = [pl.BlockSpec((tm, d), row), pl.BlockSpec((tm, d), row),
                _resident((1, d)), _resident((d, d_ff)), _resident((d_ff, d))]
    args = [x2, delta, g, w1_bf16, w2_bf16]
    if final_g is not None:
        in_specs.append(_resident((1, d)))
        args.append(final_g)
    return pl.pallas_call(
        functools.partial(_mlp_kernel, final_norm=final_g is not None),
        grid=(tokens // tm,),
        in_specs=in_specs,
        out_specs=pl.BlockSpec((tm, d), row),
        out_shape=jax.ShapeDtypeStruct((tokens, d), F32),
        compiler_params=_params("parallel"),
        name="mlp_final" if final_g is not None else "mlp",
    )(*args)


def _att_proj_kernel(x_ref, g_ref, wqt_ref, wk_ref, wvt_ref, qt_ref, k_ref, vt_ref, *, q_scale):
    d = x_ref.shape[1]
    blk = qt_ref.shape[2]
    step = 512
    rows = x_ref.shape[0]
    part = rows // ROW_PARTS
    per = part // blk
    hns = [_rmsnorm_bf16(x_ref[r:r + part, :], g_ref[...]) for r in range(0, rows, part)]
    for idx, r in enumerate(range(0, rows, part)):
        hn = hns[idx]

        def store_feature_major(out_ref, c, t):
            for s in range(per):
                out_ref[idx * per + s, c:c + step, :] = t[:, s * blk:(s + 1) * blk].astype(BF16)

        for c in range(0, d, step):
            store_feature_major(qt_ref, c, _dot_nt(wqt_ref[c:c + step, :], hn) * q_scale)
        for c in range(0, d, step):
            k_ref[r:r + part, c:c + step] = _dot(hn, wk_ref[:, c:c + step]).astype(BF16)
        for c in range(0, d, step):
            store_feature_major(vt_ref, c, _dot_nt(wvt_ref[c:c + step, :], hn))


def _att_proj(x2, g, wqt, wk, wvt):
    tokens, d = x2.shape
    tm = ROW_BLOCK
    blk = SEQ_BLOCK
    row = lambda i: (i, 0)
    slab = lambda i: (i, 0, 0)
    return pl.pallas_call(
        functools.partial(_att_proj_kernel, q_scale=(d // ATT_HEADS) ** -0.5),
        grid=(tokens // tm,),
        in_specs=[pl.BlockSpec((tm, d), row), _resident((1, d)),
                  _resident((d, d)), _resident((d, d)), _resident((d, d))],
        out_specs=[pl.BlockSpec((tm // blk, d, blk), slab), pl.BlockSpec((tm, d), row),
                   pl.BlockSpec((tm // blk, d, blk), slab)],
        out_shape=[jax.ShapeDtypeStruct((tokens // blk, d, blk), BF16),
                   jax.ShapeDtypeStruct((tokens, d), BF16),
                   jax.ShapeDtypeStruct((tokens // blk, d, blk), BF16)],
        compiler_params=_params("parallel"),
        name="att_proj",
    )(x2, g, wqt, wk, wvt)


def _band_live_pieces(blk, sub):
    m = np.arange(3 * blk)[:, None]
    n = np.arange(blk)[None, :]
    key_chunk = m // CHUNK - PAST_CHUNKS
    query_chunk = n // CHUNK
    valid = (key_chunk <= query_chunk) & (key_chunk >= query_chunk - PAST_CHUNKS)
    pieces = valid.reshape(3, blk // sub, sub, blk // sub, sub).any(axis=(2, 4))
    return pieces.tolist()


def _att_core_kernel(qt_ref, k_ref, vt_ref, rel_ref, wout_ref, o_ref, ot_ref, bias_ref,
                     khist_ref, vhist_ref, *score_refs, heads, dh):
    b = pl.program_id(0)
    j = pl.program_id(1)
    blocks_per_step, _, blk = qt_ref.shape
    n_tiles = 3

    @pl.when((b == 0) & (j == 0))
    def _build_bias():
        r = lax.broadcasted_iota(jnp.int32, (blk, blk), 0)
        c = lax.broadcasted_iota(jnp.int32, (blk, blk), 1)
        shift = CHUNK.bit_length() - 1
        query_chunk = lax.shift_right_logical(c, shift)
        chunks_per_blk = blk // CHUNK

        def per_head(h, carry):
            for t in range(n_tiles):
                key_chunk = lax.shift_right_logical(r, shift) + (t * chunks_per_blk - PAST_CHUNKS)
                valid = (key_chunk <= query_chunk) & (key_chunk >= query_chunk - PAST_CHUNKS)
                row = jnp.broadcast_to(rel_ref[h, t:t + 1, :], (blk, 2 * blk))
                tile = pltpu.roll(row, blk, 1, stride=1, stride_axis=0)[:, :blk]
                bias_ref[h, t * blk:(t + 1) * blk, :] = jnp.where(valid, tile, -jnp.inf)
            return carry

        lax.fori_loop(0, heads, per_head, 0)

    @pl.when(j == 0)
    def _reset_history():
        khist_ref[...] = jnp.zeros_like(khist_ref)
        vhist_ref[...] = jnp.zeros_like(vhist_ref)

    zeros = jnp.zeros((dh, blk), BF16)
    ones = jnp.ones((16, blk), BF16)
    sub = V7X_LANES
    n_sub = blk // sub
    live = _band_live_pieces(blk, sub)

    def block(s, carry):
        rows = pl.ds(pl.multiple_of(s * blk, blk), blk)
        n = j * blocks_per_step + s
        cur = lax.rem(n, 3)
        older = lax.rem(n + 1, 3)
        newer = lax.rem(n + 2, 3)
        exists = (n >= 2, n >= 1, None)
        unfolded_zero = jnp.minimum(n, 0)

        def k_tile(t, lo):
            if t == 2:
                return k_ref[rows, lo:lo + 2 * dh]
            return khist_ref[older if t == 0 else newer, :, lo:lo + 2 * dh]

        def v_tile(t, h):
            if t == 2:
                return vt_ref[s, h * dh:(h + 1) * dh, :]
            return vhist_ref[older if t == 0 else newer, h * dh:(h + 1) * dh, :]

        def pair_scores(hp):
            lo = hp * 2 * dh
            q_even = jnp.concatenate([qt_ref[s, lo:lo + dh, :], zeros], axis=0)
            q_odd = jnp.concatenate([zeros, qt_ref[s, lo + dh:lo + 2 * dh, :]], axis=0)
            q_pair = jnp.concatenate([q_even, q_odd], axis=1)
            k_all = jnp.concatenate([k_tile(t, lo) for t in range(n_tiles)], axis=0)
            s_all = _dot(k_all, q_pair)
            score_ref = score_refs[hp % SCORE_SLOTS]
            maxima = []
            for par in range(2):
                h = 2 * hp + par
                head_max = []
                for ci in range(n_sub):
                    cols = slice(ci * sub, (ci + 1) * sub)
                    m = None
                    for t in range(n_tiles):
                        for ri in range(n_sub):
                            if not live[t][ri][ci]:
                                continue
                            krows = slice(t * blk + ri * sub, t * blk + (ri + 1) * sub)
                            piece = (s_all[krows, par * blk + ci * sub:par * blk + (ci + 1) * sub]
                                     + bias_ref[h, krows, cols])
                            score_ref[unfolded_zero, par, krows, cols] = piece
                            m_piece = jnp.max(piece, axis=0, keepdims=True)
                            if exists[t] is not None:
                                m_piece = jnp.where(exists[t], m_piece, -jnp.inf)
                            m = m_piece if m is None else jnp.maximum(m, m_piece)
                    head_max.append(m)
                maxima.append(head_max)
            return maxima

        def head_output(h, maxima):
            score_ref, par = score_refs[(h // 2) % SCORE_SLOTS], h % 2
            p_pieces = [[[None] * n_sub for _ in range(n_sub)] for _ in range(n_tiles)]
            for ci in range(n_sub):
                cols = slice(ci * sub, (ci + 1) * sub)
                for t in range(n_tiles):
                    off = maxima[ci] if exists[t] is None else jnp.where(exists[t], maxima[ci], jnp.inf)
                    for ri in range(n_sub):
                        if live[t][ri][ci]:
                            krows = slice(t * blk + ri * sub, t * blk + (ri + 1) * sub)
                            p_pieces[t][ri][ci] = jnp.exp(
                                (score_ref[unfolded_zero, par, krows, cols] - off).astype(BF16))
                        else:
                            p_pieces[t][ri][ci] = jnp.zeros((sub, sub), BF16)
            p_all = jnp.concatenate([jnp.concatenate(prow, axis=1)
                                     for t in range(n_tiles) for prow in p_pieces[t]], axis=0)
            lhs_all = jnp.concatenate(
                [jnp.concatenate([v_tile(t, h), ones], axis=0) for t in range(n_tiles)], axis=1)
            acc = _dot(lhs_all, p_all)
            ot_ref[h * dh:(h + 1) * dh, :] = (acc[:dh] * (1.0 / acc[dh:dh + 1])).astype(BF16)

        n_pairs = heads // 2
        ahead = SCORE_SLOTS - 1
        queue = [pair_scores(hp) for hp in range(ahead)]
        for hp in range(n_pairs):
            m_cur = queue.pop(0)
            head_output(2 * hp, m_cur[0])
            if hp + ahead < n_pairs:
                queue.append(pair_scores(hp + ahead))
            head_output(2 * hp + 1, m_cur[1])
        o_ref[rows, :] = _dot_tn(ot_ref[...], wout_ref[...])
        khist_ref[cur] = k_ref[rows, :]
        vhist_ref[cur] = vt_ref[s]
        return carry

    lax.fori_loop(0, blocks_per_step, block, 0)


def _att_core(qt, k, vt, rel_rows, wout_bf16, *, batch, seq):
    tokens, d = k.shape
    blk = SEQ_BLOCK
    step = SEQ_STEP
    nstep = seq // step
    dh = d // ATT_HEADS
    row = lambda b, j: (b * nstep + j, 0)
    slab = lambda b, j: (b * nstep + j, 0, 0)
    return pl.pallas_call(
        functools.partial(_att_core_kernel, heads=ATT_HEADS, dh=dh),
        grid=(batch, nstep),
        in_specs=[
            pl.BlockSpec((step // blk, d, blk), slab),
            pl.BlockSpec((step, d), row),
            pl.BlockSpec((step // blk, d, blk), slab),
            _resident(rel_rows.shape),
            _resident((d, d)),
        ],
        out_specs=pl.BlockSpec((step, d), row),
        out_shape=jax.ShapeDtypeStruct((tokens, d), F32),
        scratch_shapes=[pltpu.VMEM((d, blk), BF16),
                        pltpu.VMEM((ATT_HEADS, 3 * blk, blk), F32),
                        pltpu.VMEM((3, blk, d), BF16),
                        pltpu.VMEM((3, d, blk), BF16),
                        ] + [pltpu.VMEM((1, 2, 3 * blk, blk), F32) for _ in range(SCORE_SLOTS)],
        compiler_params=_params("arbitrary", "arbitrary"),
        name="att_core",
    )(qt, k, vt, rel_rows, wout_bf16)


def _attention_rel_rows(rel_bias):
    blk = SEQ_BLOCK
    assert PAST_CHUNKS * CHUNK == 2 * blk
    a = np.arange(2 * blk)[None, :]
    t = np.arange(3)[:, None]
    rel = blk * (t - 1) - a
    idx = np.clip(np.maximum(rel, -MAX_REL) + MAX_REL, 0, rel_bias.shape[1] - 1)
    return rel_bias.astype(F32)[:, idx]


def kernel(x, mix_norm_g, ret_w_in, ret_gn_g, ret_w_out, att_w_in, att_rel_bias, att_w_out,
           mlp_norm_g, mlp_w1, mlp_w2, final_norm_g):
    batch, seq, d = x.shape
    depth = mix_norm_g.shape[0]
    assert seq % ROW_BLOCK == 0 and seq % SEQ_STEP == 0
    assert ROW_BLOCK % (ROW_PARTS * SEQ_BLOCK) == 0 and SEQ_STEP % SEQ_BLOCK == 0 and SEQ_BLOCK % CHUNK == 0
    h = x.reshape(batch * seq, d)
    row_vec = lambda v: v.reshape(1, -1).astype(F32)

    for i in range(depth):
        jdx = i // 2
        if i % 2 == 0:
            w_in = ret_w_in[jdx].astype(BF16)
            dk = d // RET_HEADS
            cos, sin = _rope_tables(seq, dk)
            mask, qdec, kdec, block_decay = _retention_tables(dk)
            q, k, v, gate = _ret_proj(h, row_vec(mix_norm_g[i]), w_in, cos, sin, qdec, kdec,
                                      row_vec(ret_gn_g[jdx]), seq=seq)
            delta = _ret_core(q, k, v, gate, mask, block_decay, ret_w_out[jdx].astype(BF16),
                              batch=batch, seq=seq)
        else:
            w_in = att_w_in[jdx]
            wqt = w_in[:, :d].T.astype(BF16)
            wk = w_in[:, d:2 * d].astype(BF16)
            wvt = w_in[:, 2 * d:].T.astype(BF16)
            qt, k, vt = _att_proj(h, row_vec(mix_norm_g[i]), wqt, wk, wvt)
            delta = _att_core(qt, k, vt, _attention_rel_rows(att_rel_bias[jdx]),
                              att_w_out[jdx].astype(BF16), batch=batch, seq=seq)
        last = i == depth - 1
        h = _mlp(h, delta, row_vec(mlp_norm_g[i]), mlp_w1[i].astype(BF16), mlp_w2[i].astype(BF16),
                 row_vec(final_norm_g) if last else None)
    if depth == 0:
        raise NotImplementedError("depth 0 has no Pallas stage")
    return h.reshape(batch, seq, d)
```

```python
import functools

import numpy as np
import jax
import jax.numpy as jnp
from jax import lax
from jax.experimental import pallas as pl
from jax.experimental.pallas import tpu as pltpu

CHUNK = 64
RET_HEADS = 4
ATT_HEADS = 16
PAST_CHUNKS = 8
MAX_REL = 256
ROPE_BASE = 10000.0
EPS = 1e-6

V7X_LANES = 128
V7X_VMEM_LIMIT_BYTES = 60000 * 1024

ROW_BLOCK = 1024
ROW_PARTS = 4
SEQ_BLOCK = 256
SEQ_STEP = 1024
SCORE_SLOTS = 3
FF_BLOCK = 1024

BF16 = jnp.bfloat16
F32 = jnp.float32


def _params(*semantics):
    return pltpu.CompilerParams(dimension_semantics=semantics,
                                vmem_limit_bytes=V7X_VMEM_LIMIT_BYTES)


def _resident(shape):
    zeros = (0,) * len(shape)
    return pl.BlockSpec(shape, lambda *_: zeros, pipeline_mode=pl.Buffered(1))


def _rmsnorm_bf16(x, g):
    ms = jnp.mean(x * x, axis=-1, keepdims=True)
    return (x * lax.rsqrt(ms + EPS) * g).astype(BF16)


def _dot(a, b):
    return jnp.dot(a, b, preferred_element_type=F32)


def _dot_nt(a, b):
    return lax.dot_general(a, b, (((1,), (1,)), ((), ())), preferred_element_type=F32)


def _dot_tn(a, b):
    return lax.dot_general(a, b, (((0,), (0,)), ((), ())), preferred_element_type=F32)


def _ret_proj_kernel(x_ref, g_ref, w_ref, cos_ref, sin_ref, qdec_ref, kdec_ref, gn_ref,
                     q_ref, k_ref, v_ref, gate_ref, *, heads, dk):
    hn = _rmsnorm_bf16(x_ref[...], g_ref[...])
    cos = cos_ref[...]
    sin = sin_ref[...]
    half = dk // 2
    qk_width = heads * dk

    def rope_store(out_ref, col0, dec_ref):
        for h in range(heads):
            t = _dot(hn, w_ref[:, col0 + h * dk: col0 + (h + 1) * dk])
            t1 = t[:, :half]
            t2 = t[:, half:]
            dec = dec_ref[h]
            out_ref[:, h * dk: h * dk + half] = ((t1 * cos - t2 * sin) * dec).astype(BF16)
            out_ref[:, h * dk + half: (h + 1) * dk] = ((t1 * sin + t2 * cos) * dec).astype(BF16)

    vw = v_ref.shape[1]
    step = 512
    for c in range(0, vw, step):
        gate = _dot(hn, w_ref[:, 2 * qk_width + vw + c: 2 * qk_width + vw + c + step])
        silu2 = gate * (1.0 + jnp.tanh(0.5 * gate))
        gate_ref[:, c:c + step] = (silu2 * (0.5 * gn_ref[:, c:c + step])).astype(BF16)
    rope_store(q_ref, 0, qdec_ref)
    rope_store(k_ref, qk_width, kdec_ref)
    for c in range(0, vw, step):
        v_ref[:, c:c + step] = _dot(hn, w_ref[:, 2 * qk_width + c: 2 * qk_width + c + step]).astype(BF16)


def _ret_proj(x2, g, w_bf16, cos, sin, qdec, kdec, gn_g, *, seq):
    tokens, d = x2.shape
    n_all = w_bf16.shape[1]
    qk = d
    vw = (n_all - 2 * qk) // 2
    dk = qk // RET_HEADS
    tm = ROW_BLOCK
    blocks_per_seq = seq // tm
    row = lambda i: (i, 0)
    pos = lambda i: (i % blocks_per_seq, 0)
    return pl.pallas_call(
        functools.partial(_ret_proj_kernel, heads=RET_HEADS, dk=dk),
        grid=(tokens // tm,),
        in_specs=[
            pl.BlockSpec((tm, d), row),
            _resident((1, d)),
            _resident((d, n_all)),
            pl.BlockSpec((tm, dk // 2), pos),
            pl.BlockSpec((tm, dk // 2), pos),
            _resident(qdec.shape),
            _resident(kdec.shape),
            _resident((1, vw)),
        ],
        out_specs=[
            pl.BlockSpec((tm, qk), row),
            pl.BlockSpec((tm, qk), row),
            pl.BlockSpec((tm, vw), row),
            pl.BlockSpec((tm, vw), row),
        ],
        out_shape=[
            jax.ShapeDtypeStruct((tokens, qk), BF16),
            jax.ShapeDtypeStruct((tokens, qk), BF16),
            jax.ShapeDtypeStruct((tokens, vw), BF16),
            jax.ShapeDtypeStruct((tokens, vw), BF16),
        ],
        compiler_params=_params("parallel"),
        name="ret_proj",
    )(x2, g, w_bf16, cos, sin, qdec, kdec, gn_g)


def _ret_core_kernel(q_ref, k_ref, v_ref, gate_ref, mask_ref, wout_ref, o_ref, state_ref,
                     *, heads, dk, dv, blk, block_decay):
    @pl.when(pl.program_id(1) == 0)
    def _():
        state_ref[...] = jnp.zeros_like(state_ref)

    def block(s, carry):
        rows = pl.ds(pl.multiple_of(s * blk, blk), blk)

        def retention(h):
            qh = q_ref[rows, h * dk:(h + 1) * dk]
            kh = k_ref[rows, h * dk:(h + 1) * dk]
            vh = v_ref[rows, h * dv:(h + 1) * dv]
            state = state_ref[h]
            scores = _dot_nt(qh, kh) * mask_ref[h]
            o = _dot(scores.astype(BF16), vh) + _dot(qh, state.astype(BF16))
            state_ref[h] = state * block_decay[h] + _dot_tn(kh, vh)
            return o

        def normed_gated(h, o):
            mu = jnp.mean(o, axis=-1, keepdims=True)
            cen = o - mu
            var = jnp.mean(cen * cen, axis=-1, keepdims=True)
            y = cen * lax.rsqrt(var + EPS) * gate_ref[rows, h * dv:(h + 1) * dv].astype(F32)
            return y.astype(BF16)

        acc = None
        o_next = retention(0)
        for h in range(heads):
            o_cur = o_next
            if h + 1 < heads:
                o_next = retention(h + 1)
            part = _dot(normed_gated(h, o_cur), wout_ref[h * dv:(h + 1) * dv, :])
            acc = part if acc is None else acc + part
        o_ref[rows, :] = acc
        return carry

    lax.fori_loop(0, q_ref.shape[0] // blk, block, 0)


def _ret_core(q, k, v, gate, mask, block_decay, wout_bf16, *, batch, seq):
    tokens, qk = q.shape
    d = wout_bf16.shape[1]
    vw = v.shape[1]
    dk = qk // RET_HEADS
    dv = vw // RET_HEADS
    step = SEQ_STEP
    nstep = seq // step
    row = lambda b, j: (b * nstep + j, 0)
    return pl.pallas_call(
        functools.partial(_ret_core_kernel, heads=RET_HEADS, dk=dk, dv=dv, blk=SEQ_BLOCK,
                          block_decay=block_decay),
        grid=(batch, nstep),
        in_specs=[
            pl.BlockSpec((step, qk), row),
            pl.BlockSpec((step, qk), row),
            pl.BlockSpec((step, vw), row),
            pl.BlockSpec((step, vw), row),
            _resident(mask.shape),
            _resident((vw, d)),
        ],
        out_specs=pl.BlockSpec((step, d), row),
        out_shape=jax.ShapeDtypeStruct((tokens, d), F32),
        scratch_shapes=[pltpu.VMEM((RET_HEADS, dk, dv), F32)],
        compiler_params=_params("parallel", "arbitrary"),
        name="ret_core",
    )(q, k, v, gate, mask, wout_bf16)


def _retention_tables(dk):
    blk = SEQ_BLOCK
    log_gamma = np.log1p(-np.exp2(-5.0 - np.arange(RET_HEADS, dtype=np.float64)))
    n = np.arange(blk)
    diff = n[:, None] - n[None, :]
    same = (n[:, None] // CHUNK) == (n[None, :] // CHUNK)
    earlier = (n[None, :] // CHUNK) < (n[:, None] // CHUNK)
    lg = log_gamma[:, None, None]
    log_mask = np.where(same[None], lg * np.abs(diff)[None], lg * diff[None])
    log_qk = lg * ((n + 1.0)[:, None] + (blk - 1.0 - n)[None, :])[None]
    mask = np.where((same | earlier)[None], np.exp(log_mask - log_qk), 0.0)
    qdec = np.exp(log_gamma[:, None] * (n + 1.0)[None, :])
    kdec = np.exp(log_gamma[:, None] * (blk - 1.0 - n)[None, :]) * dk ** -0.5
    block_decay = tuple(float(c) for c in np.exp(log_gamma * blk))
    reps = ROW_BLOCK // blk
    lanes = dk // 2
    per_row = lambda a: np.broadcast_to(np.tile(a, (1, reps))[:, :, None], (RET_HEADS, ROW_BLOCK, lanes))
    as_f32 = lambda a: jnp.asarray(np.ascontiguousarray(a), dtype=F32)
    return as_f32(mask), as_f32(per_row(qdec)), as_f32(per_row(kdec)), block_decay


def _rope_tables(seq, dk):
    half = dk // 2
    inv = np.exp(-np.log(ROPE_BASE) * np.arange(half, dtype=np.float64) / half)
    ang = np.arange(seq, dtype=np.float64)[:, None] * inv[None, :]
    return jnp.asarray(np.cos(ang), dtype=F32), jnp.asarray(np.sin(ang), dtype=F32)


def _mlp_kernel(x_ref, delta_ref, g_ref, w1_ref, w2_ref, *rest, final_norm):
    if final_norm:
        gf_ref, o_ref = rest
    else:
        (o_ref,) = rest
    d_ff = w1_ref.shape[1]
    rows = x_ref.shape[0]
    part = rows // ROW_PARTS
    hs = [x_ref[r:r + part, :] + delta_ref[r:r + part, :] for r in range(0, rows, part)]
    hns = [_rmsnorm_bf16(h, g_ref[...]) for h in hs]
    for idx, r in enumerate(range(0, rows, part)):
        hn = hns[idx]
        acc = hs[idx]
        for c in range(0, d_ff, FF_BLOCK):
            u = jnp.maximum(_dot(hn, w1_ref[:, c:c + FF_BLOCK]), 0.0)
            acc = acc + _dot((u * u).astype(BF16), w2_ref[c:c + FF_BLOCK, :])
        if final_norm:
            ms = jnp.mean(acc * acc, axis=-1, keepdims=True)
            acc = acc * lax.rsqrt(ms + EPS) * gf_ref[...]
        o_ref[r:r + part, :] = acc


def _mlp(x2, delta, g, w1_bf16, w2_bf16, final_g=None):
    tokens, d = x2.shape
    d_ff = w1_bf16.shape[1]
    tm = ROW_BLOCK
    row = lambda i: (i, 0)
    in_specs = [pl.BlockSpec((tm, d), row), pl.BlockSpec((tm, d), row),
                _resident((1, d)), _resident((d, d_ff)), _resident((d_ff, d))]
    args = [x2, delta, g, w1_bf16, w2_bf16]
    if final_g is not None:
        in_specs.append(_resident((1, d)))
        args.append(final_g)
    return pl.pallas_call(
        functools.partial(_mlp_kernel, final_norm=final_g is not None),
        grid=(tokens // tm,),
        in_specs=in_specs,
        out_specs=pl.BlockSpec((tm, d), row),
        out_shape=jax.ShapeDtypeStruct((tokens, d), F32),
        compiler_params=_params("parallel"),
        name="mlp_final" if final_g is not None else "mlp",
    )(*args)


def _att_proj_kernel(x_ref, g_ref, wqt_ref, wk_ref, wvt_ref, qt_ref, k_ref, vt_ref, *, q_scale):
    d = x_ref.shape[1]
    blk = qt_ref.shape[2]
    step = 512
    rows = x_ref.shape[0]
    part = rows // ROW_PARTS
    per = part // blk
    hns = [_rmsnorm_bf16(x_ref[r:r + part, :], g_ref[...]) for r in range(0, rows, part)]
    for idx, r in enumerate(range(0, rows, part)):
        hn = hns[idx]

        def store_feature_major(out_ref, c, t):
            for s in range(per):
                out_ref[idx * per + s, c:c + step, :] = t[:, s * blk:(s + 1) * blk].astype(BF16)

        for c in range(0, d, step):
            store_feature_major(qt_ref, c, _dot_nt(wqt_ref[c:c + step, :], hn) * q_scale)
        for c in range(0, d, step):
            k_ref[r:r + part, c:c + step] = _dot(hn, wk_ref[:, c:c + step]).astype(BF16)
        for c in range(0, d, step):
            store_feature_major(vt_ref, c, _dot_nt(wvt_ref[c:c + step, :], hn))


def _att_proj(x2, g, wqt, wk, wvt):
    tokens, d = x2.shape
    tm = ROW_BLOCK
    blk = SEQ_BLOCK
    row = lambda i: (i, 0)
    slab = lambda i: (i, 0, 0)
    return pl.pallas_call(
        functools.partial(_att_proj_kernel, q_scale=(d // ATT_HEADS) ** -0.5),
        grid=(tokens // tm,),
        in_specs=[pl.BlockSpec((tm, d), row), _resident((1, d)),
                  _resident((d, d)), _resident((d, d)), _resident((d, d))],
        out_specs=[pl.BlockSpec((tm // blk, d, blk), slab), pl.BlockSpec((tm, d), row),
                   pl.BlockSpec((tm // blk, d, blk), slab)],
        out_shape=[jax.ShapeDtypeStruct((tokens // blk, d, blk), BF16),
                   jax.ShapeDtypeStruct((tokens, d), BF16),
                   jax.ShapeDtypeStruct((tokens // blk, d, blk), BF16)],
        compiler_params=_params("parallel"),
        name="att_proj",
    )(x2, g, wqt, wk, wvt)


def _band_live_pieces(blk, sub):
    m = np.arange(3 * blk)[:, None]
    n = np.arange(blk)[None, :]
    key_chunk = m // CHUNK - PAST_CHUNKS
    query_chunk = n // CHUNK
    valid = (key_chunk <= query_chunk) & (key_chunk >= query_chunk - PAST_CHUNKS)
    pieces = valid.reshape(3, blk // sub, sub, blk // sub, sub).any(axis=(2, 4))
    return pieces.tolist()


def _att_core_kernel(qt_ref, k_ref, vt_ref, rel_ref, wout_ref, o_ref, ot_ref, bias_ref,
                     khist_ref, vhist_ref, *score_refs, heads, dh):
    b = pl.program_id(0)
    j = pl.program_id(1)
    blocks_per_step, _, blk = qt_ref.shape
    n_tiles = 3

    @pl.when((b == 0) & (j == 0))
    def _build_bias():
        r = lax.broadcasted_iota(jnp.int32, (blk, blk), 0)
        c = lax.broadcasted_iota(jnp.int32, (blk, blk), 1)
        shift = CHUNK.bit_length() - 1
        query_chunk = lax.shift_right_logical(c, shift)
        chunks_per_blk = blk // CHUNK

        def per_head(h, carry):
            for t in range(n_tiles):
                key_chunk = lax.shift_right_logical(r, shift) + (t * chunks_per_blk - PAST_CHUNKS)
                valid = (key_chunk <= query_chunk) & (key_chunk >= query_chunk - PAST_CHUNKS)
                row = jnp.broadcast_to(rel_ref[h, t:t + 1, :], (blk, 2 * blk))
                tile = pltpu.roll(row, blk, 1, stride=1, stride_axis=0)[:, :blk]
                bias_ref[h, t * blk:(t + 1) * blk, :] = jnp.where(valid, tile, -jnp.inf)
            return carry

        lax.fori_loop(0, heads, per_head, 0)

    @pl.when(j == 0)
    def _reset_history():
        khist_ref[...] = jnp.zeros_like(khist_ref)
        vhist_ref[...] = jnp.zeros_like(vhist_ref)

    zeros = jnp.zeros((dh, blk), BF16)
    ones = jnp.ones((16, blk), BF16)
    sub = V7X_LANES
    n_sub = blk // sub
    live = _band_live_pieces(blk, sub)
    live_rows = [[t * blk + ri * sub for t in range(n_tiles) for ri in range(n_sub) if live[t][ri][ci]]
                 for ci in range(n_sub)]
    key_lo = [min(r) for r in live_rows]
    key_hi = [max(r) + sub for r in live_rows]

    def block(s, carry):
        rows = pl.ds(pl.multiple_of(s * blk, blk), blk)
        n = j * blocks_per_step + s
        cur = lax.rem(n, 3)
        older = lax.rem(n + 1, 3)
        newer = lax.rem(n + 2, 3)
        exists = (n >= 2, n >= 1, None)
        unfolded_zero = jnp.minimum(n, 0)

        def k_tile(t, lo):
            if t == 2:
                return k_ref[rows, lo:lo + 2 * dh]
            return khist_ref[older if t == 0 else newer, :, lo:lo + 2 * dh]

        def v_tile(t, h):
            if t == 2:
                return vt_ref[s, h * dh:(h + 1) * dh, :]
            return vhist_ref[older if t == 0 else newer, h * dh:(h + 1) * dh, :]

        def pair_scores(hp):
            lo = hp * 2 * dh
            q_even = jnp.concatenate([qt_ref[s, lo:lo + dh, :], zeros], axis=0)
            q_odd = jnp.concatenate([zeros, qt_ref[s, lo + dh:lo + 2 * dh, :]], axis=0)
            k_all = jnp.concatenate([k_tile(t, lo) for t in range(n_tiles)], axis=0)
            s_cols = []
            for ci in range(n_sub):
                cs = slice(ci * sub, (ci + 1) * sub)
                q_cols = jnp.concatenate([q_even[:, cs], q_odd[:, cs]], axis=1)
                s_cols.append(_dot(k_all[key_lo[ci]:key_hi[ci]], q_cols))
            score_ref = score_refs[hp % SCORE_SLOTS]
            maxima = []
            for par in range(2):
                h = 2 * hp + par
                head_max = []
                for ci in range(n_sub):
                    cols = slice(ci * sub, (ci + 1) * sub)
                    m = None
                    for t in range(n_tiles):
                        for ri in range(n_sub):
                            if not live[t][ri][ci]:
                                continue
                            krows = slice(t * blk + ri * sub, t * blk + (ri + 1) * sub)
                            key0 = t * blk + ri * sub - key_lo[ci]
                            piece = (s_cols[ci][key0:key0 + sub, par * sub:(par + 1) * sub]
                                     + bias_ref[h, krows, cols])
                            score_ref[unfolded_zero, par, krows, cols] = piece
                            m_piece = jnp.max(piece, axis=0, keepdims=True)
                            if exists[t] is not None:
                                m_piece = jnp.where(exists[t], m_piece, -jnp.inf)
                            m = m_piece if m is None else jnp.maximum(m, m_piece)
                    head_max.append(m)
                maxima.append(head_max)
            return maxima

        def head_output(h, maxima):
            score_ref, par = score_refs[(h // 2) % SCORE_SLOTS], h % 2
            p_pieces = [[[None] * n_sub for _ in range(n_sub)] for _ in range(n_tiles)]
            for ci in range(n_sub):
                cols = slice(ci * sub, (ci + 1) * sub)
                for t in range(n_tiles):
                    off = maxima[ci] if exists[t] is None else jnp.where(exists[t], maxima[ci], jnp.inf)
                    for ri in range(n_sub):
                        if live[t][ri][ci]:
                            krows = slice(t * blk + ri * sub, t * blk + (ri + 1) * sub)
                            p_pieces[t][ri][ci] = jnp.exp(
                                (score_ref[unfolded_zero, par, krows, cols] - off).astype(BF16))
                        else:
                            p_pieces[t][ri][ci] = jnp.zeros((sub, sub), BF16)
            p_all = jnp.concatenate([jnp.concatenate(prow, axis=1)
                                     for t in range(n_tiles) for prow in p_pieces[t]], axis=0)
            lhs_all = jnp.concatenate(
                [jnp.concatenate([v_tile(t, h), ones], axis=0) for t in range(n_tiles)], axis=1)
            acc = _dot(lhs_all, p_all)
            ot_ref[h * dh:(h + 1) * dh, :] = (acc[:dh] * (1.0 / acc[dh:dh + 1])).astype(BF16)

        n_pairs = heads // 2
        ahead = SCORE_SLOTS - 1
        queue = [pair_scores(hp) for hp in range(ahead)]
        for hp in range(n_pairs):
            m_cur = queue.pop(0)
            head_output(2 * hp, m_cur[0])
            if hp + ahead < n_pairs:
                queue.append(pair_scores(hp + ahead))
            head_output(2 * hp + 1, m_cur[1])
        o_ref[rows, :] = _dot_tn(ot_ref[...], wout_ref[...])
        khist_ref[cur] = k_ref[rows, :]
        vhist_ref[cur] = vt_ref[s]
        return carry

    lax.fori_loop(0, blocks_per_step, block, 0)


def _att_core(qt, k, vt, rel_rows, wout_bf16, *, batch, seq):
    tokens, d = k.shape
    blk = SEQ_BLOCK
    step = SEQ_STEP
    nstep = seq // step
    dh = d // ATT_HEADS
    row = lambda b, j: (b * nstep + j, 0)
    slab = lambda b, j: (b * nstep + j, 0, 0)
    return pl.pallas_call(
        functools.partial(_att_core_kernel, heads=ATT_HEADS, dh=dh),
        grid=(batch, nstep),
        in_specs=[
            pl.BlockSpec((step // blk, d, blk), slab),
            pl.BlockSpec((step, d), row),
            pl.BlockSpec((step // blk, d, blk), slab),
            _resident(rel_rows.shape),
            _resident((d, d)),
        ],
        out_specs=pl.BlockSpec((step, d), row),
        out_shape=jax.ShapeDtypeStruct((tokens, d), F32),
        scratch_shapes=[pltpu.VMEM((d, blk), BF16),
                        pltpu.VMEM((ATT_HEADS, 3 * blk, blk), F32),
                        pltpu.VMEM((3, blk, d), BF16),
                        pltpu.VMEM((3, d, blk), BF16),
                        ] + [pltpu.VMEM((1, 2, 3 * blk, blk), F32) for _ in range(SCORE_SLOTS)],
        compiler_params=_params("arbitrary", "arbitrary"),
        name="att_core",
    )(qt, k, vt, rel_rows, wout_bf16)


def _attention_rel_rows(rel_bias):
    blk = SEQ_BLOCK
    assert PAST_CHUNKS * CHUNK == 2 * blk
    a = np.arange(2 * blk)[None, :]
    t = np.arange(3)[:, None]
    rel = blk * (t - 1) - a
    idx = np.clip(np.maximum(rel, -MAX_REL) + MAX_REL, 0, rel_bias.shape[1] - 1)
    return rel_bias.astype(F32)[:, idx]


def kernel(x, mix_norm_g, ret_w_in, ret_gn_g, ret_w_out, att_w_in, att_rel_bias, att_w_out,
           mlp_norm_g, mlp_w1, mlp_w2, final_norm_g):
    batch, seq, d = x.shape
    depth = mix_norm_g.shape[0]
    assert seq % ROW_BLOCK == 0 and seq % SEQ_STEP == 0
    assert ROW_BLOCK % (ROW_PARTS * SEQ_BLOCK) == 0 and SEQ_STEP % SEQ_BLOCK == 0 and SEQ_BLOCK % CHUNK == 0
    h = x.reshape(batch * seq, d)
    row_vec = lambda v: v.reshape(1, -1).astype(F32)

    for i in range(depth):
        jdx = i // 2
        if i % 2 == 0:
            w_in = ret_w_in[jdx].astype(BF16)
            dk = d // RET_HEADS
            cos, sin = _rope_tables(seq, dk)
            mask, qdec, kdec, block_decay = _retention_tables(dk)
            q, k, v, gate = _ret_proj(h, row_vec(mix_norm_g[i]), w_in, cos, sin, qdec, kdec,
                                      row_vec(ret_gn_g[jdx]), seq=seq)
            delta = _ret_core(q, k, v, gate, mask, block_decay, ret_w_out[jdx].astype(BF16),
                              batch=batch, seq=seq)
        else:
            w_in = att_w_in[jdx]
            wqt = w_in[:, :d].T.astype(BF16)
            wk = w_in[:, d:2 * d].astype(BF16)
            wvt = w_in[:, 2 * d:].T.astype(BF16)
            qt, k, vt = _att_proj(h, row_vec(mix_norm_g[i]), wqt, wk, wvt)
            delta = _att_core(qt, k, vt, _attention_rel_rows(att_rel_bias[jdx]),
                              att_w_out[jdx].astype(BF16), batch=batch, seq=seq)
        last = i == depth - 1
        h = _mlp(h, delta, row_vec(mlp_norm_g[i]), mlp_w1[i].astype(BF16), mlp_w2[i].astype(BF16),
                 row_vec(final_norm_g) if last else None)
    if depth == 0:
        raise NotImplementedError("depth 0 has no Pallas stage")
    return h.reshape(batch, seq, d)
```
